```python
import jax, jax.numpy as jnp
from jax import lax
import numpy as np

D_MODEL = 1024
BATCH = 8
SEQ = 2048
DEPTH = 1
DEC_BATCH = 128
DEC_SEQ = 4
PAST_LEN = 16384
PAGE_SIZE = 128

MIX_WIDTH = D_MODEL
POOL_WIDTH = MIX_WIDTH // 2
POOL_WINDOWS = (2, 4, 8, 16)
POOL_GROUPS = len(POOL_WINDOWS)
POOL_GROUP_DIM = POOL_WIDTH // POOL_GROUPS
POOL_BUF = max(POOL_WINDOWS) - 1
SGU_WIDTH = MIX_WIDTH - POOL_WIDTH
SGU_HEADS = 4
SGU_HEAD_DIM = SGU_WIDTH // SGU_HEADS
CHUNK = 128
IN_WIDTH = POOL_WIDTH + 2 * SGU_WIDTH
N_EXPERTS = 32
TOP_K = 4
D_FF = D_MODEL
SWIGLU_LIMIT = 7.0
SWIGLU_ALPHA = 1.702
MOE_BLOCK = 128
PLE_DIM = 256
RMS_EPS = 1e-6
LN_EPS = 1e-5

kernel_name = "hybrid_pool_sgu_moe_decoder_step"


def rmsnorm(x, g):
    xf = x.astype(jnp.float32)
    y = xf * lax.rsqrt(jnp.mean(xf * xf, axis=-1, keepdims=True) + RMS_EPS)
    return (y * g.astype(jnp.float32)).astype(x.dtype)


def head_layernorm(v, g, b):
    vf = v.astype(jnp.float32)
    mu = jnp.mean(vf, axis=-1, keepdims=True)
    var = jnp.mean(jnp.square(vf - mu), axis=-1, keepdims=True)
    y = (vf - mu) * lax.rsqrt(var + LN_EPS) * g.astype(jnp.float32) + b.astype(jnp.float32)
    return y.astype(v.dtype)


def pool_mix(z, prefix, pos0, w_pool, pool_scale):
    B, L, _ = z.shape
    ext = jnp.concatenate([prefix.astype(z.dtype), z], axis=1)
    cs = jnp.cumsum(ext.astype(jnp.float32), axis=1)
    cs = jnp.pad(cs, ((0, 0), (1, 0), (0, 0)))
    pos = pos0 + jnp.arange(L)
    outs = []
    for g, w in enumerate(POOL_WINDOWS):
        cols = slice(g * POOL_GROUP_DIM, (g + 1) * POOL_GROUP_DIM)
        hi = cs[:, POOL_BUF + 1:POOL_BUF + 1 + L, cols]
        lo = cs[:, POOL_BUF + 1 - w:POOL_BUF + 1 - w + L, cols]
        cnt = jnp.minimum(w, pos + 1).astype(jnp.float32)[None, :, None]
        outs.append((hi - lo) / cnt)
    pooled = jnp.stack(outs, axis=2)
    diff = pooled - z.astype(jnp.float32).reshape(B, L, POOL_GROUPS, POOL_GROUP_DIM)
    y = jnp.einsum('blgc,gcd->blgd', diff.astype(z.dtype), w_pool) * pool_scale
    return y.reshape(B, L, POOL_WIDTH), ext[:, -POOL_BUF:]


def sgu_mix(v, w_s, b_s):
    B, L, H, dh = v.shape
    n_chunks = -(-L // CHUNK)
    pad = n_chunks * CHUNK - L
    vp = jnp.pad(v, ((0, 0), (0, pad), (0, 0), (0, 0))).reshape(B, n_chunks, CHUNK, H, dh)
    mask = jnp.tril(jnp.ones((CHUNK, CHUNK), dtype=bool))
    ws = jnp.where(mask[None], w_s, jnp.zeros_like(w_s))
    mixed = jnp.einsum('hij,bcjhd->bcihd', ws, vp) + b_s.T[None, None, :, :, None]
    return mixed.reshape(B, n_chunks * CHUNK, H, dh)[:, :L]


def mixer(h, pool_prefix, pos0, w_in, w_pool, pool_scale, sgu_ln_g, sgu_ln_b, sgu_w, sgu_b, w_out):
    B, L, _ = h.shape
    z = h @ w_in
    z_pool = z[..., :POOL_WIDTH]
    z_uv = jax.nn.gelu(z[..., POOL_WIDTH:], approximate=False)
    u = z_uv[..., :SGU_WIDTH].reshape(B, L, SGU_HEADS, SGU_HEAD_DIM)
    v = z_uv[..., SGU_WIDTH:].reshape(B, L, SGU_HEADS, SGU_HEAD_DIM)
    a_out, pool_buf = pool_mix(z_pool, pool_prefix, pos0, w_pool, pool_scale)
    v = head_layernorm(v, sgu_ln_g, sgu_ln_b)
    b_out = (u * sgu_mix(v, sgu_w, sgu_b)).reshape(B, L, SGU_WIDTH)
    mix = jnp.concatenate([a_out, b_out], axis=-1) @ w_out
    return mix, pool_buf, v.reshape(B, L, SGU_WIDTH)


def moe(h, w_router, b_router, w_up, b_up, w_down, b_down):
    B, L, D = h.shape
    T = B * L
    x = h.reshape(T, D)
    logits = (x @ w_router + b_router).astype(jnp.float32)
    top_logit, top_idx = lax.top_k(logits, TOP_K)
    top_w = jax.nn.softmax(top_logit, axis=-1).astype(x.dtype)
    A = T * TOP_K
    flat_e = top_idx.reshape(-1)
    flat_tok = jnp.arange(A) // TOP_K
    flat_w = top_w.reshape(-1)
    order = jnp.argsort(flat_e)
    sorted_e = flat_e[order]
    counts = jnp.bincount(flat_e, length=N_EXPERTS)
    padded = (counts + MOE_BLOCK - 1) // MOE_BLOCK * MOE_BLOCK
    pad_end = jnp.cumsum(padded)
    pad_start = pad_end - padded
    sort_start = jnp.cumsum(counts) - counts
    dest = pad_start[sorted_e] + (jnp.arange(A) - sort_start[sorted_e])
    n_blocks = -(-A // MOE_BLOCK) + N_EXPERTS
    n_slots = n_blocks * MOE_BLOCK
    slot_tok = jnp.full((n_slots,), T, dtype=jnp.int32).at[dest].set(flat_tok[order].astype(jnp.int32))
    slot_w = jnp.zeros((n_slots,), x.dtype).at[dest].set(flat_w[order])
    block_e = jnp.minimum(jnp.searchsorted(pad_end, jnp.arange(n_blocks) * MOE_BLOCK, side='right'),
                          N_EXPERTS - 1)
    x_pad = jnp.concatenate([x, jnp.zeros((1, D), x.dtype)], axis=0)
    xb = x_pad[slot_tok].reshape(n_blocks, MOE_BLOCK, D)

    def expert_block(args):
        xe, e = args
        gu = xe @ w_up[e] + b_up[e]
        glu = jnp.minimum(gu[:, :D_FF], SWIGLU_LIMIT)
        lin = jnp.clip(gu[:, D_FF:], -SWIGLU_LIMIT, SWIGLU_LIMIT)
        act = glu * jax.nn.sigmoid(SWIGLU_ALPHA * glu) * (lin + 1.0)
        return act @ w_down[e] + b_down[e]

    yb = lax.map(expert_block, (xb, block_e))
    y = jnp.zeros((T + 1, D), x.dtype).at[slot_tok].add(yb.reshape(n_slots, D) * slot_w[:, None])
    return y[:T].reshape(B, L, D)


def layer(x, p, pool_prefix, pos0, norm_mix, w_in, w_pool, pool_scale, sgu_ln_g, sgu_ln_b,
          sgu_w, sgu_b, w_out, norm_ffn, w_router, b_router, w_up, b_up, w_down, b_down,
          w_ple_gate, w_ple_proj):
    mix, pool_buf, v = mixer(rmsnorm(x, norm_mix), pool_prefix, pos0, w_in, w_pool, pool_scale,
                             sgu_ln_g, sgu_ln_b, sgu_w, sgu_b, w_out)
    x = x + mix
    x = x + moe(rmsnorm(x, norm_ffn), w_router, b_router, w_up, b_up, w_down, b_down)
    x = x + jax.nn.sigmoid(x @ w_ple_gate) * (p.astype(x.dtype) @ w_ple_proj)
    return x, pool_buf, v


def setup_inputs(seed: int = 0) -> dict:
    key = jax.random.key(seed)
    ks = jax.random.split(key, 32)
    f32 = jnp.float32
    nrm = lambda k, shape, scale: jax.random.normal(k, shape, f32) * scale
    return {
        "x_prompt": nrm(ks[0], (BATCH, SEQ, D_MODEL), 1.0),
        "x_sample": nrm(ks[1], (DEC_BATCH, DEC_SEQ, D_MODEL), 1.0),
        "state_pool": nrm(ks[2], (DEPTH, DEC_BATCH, POOL_BUF, POOL_WIDTH), 1.0),
        "p_prompt": nrm(ks[3], (DEPTH, BATCH, SEQ, PLE_DIM), 1.0),
        "p_sample": nrm(ks[4], (DEPTH, DEC_BATCH, DEC_SEQ, PLE_DIM), 1.0),
        "norm_mix": 1.0 + nrm(ks[5], (DEPTH, D_MODEL), 0.01),
        "w_in": nrm(ks[6], (DEPTH, D_MODEL, IN_WIDTH), D_MODEL ** -0.5),
        "w_pool": nrm(ks[7], (DEPTH, POOL_GROUPS, POOL_GROUP_DIM, POOL_GROUP_DIM), POOL_GROUP_DIM ** -0.5),
        "pool_scale": 1.0 + nrm(ks[8], (DEPTH, POOL_GROUPS, POOL_GROUP_DIM), 0.1),
        "sgu_ln_g": 1.0 + nrm(ks[9], (DEPTH, SGU_HEADS, SGU_HEAD_DIM), 0.01),
        "sgu_ln_b": nrm(ks[10], (DEPTH, SGU_HEADS, SGU_HEAD_DIM), 0.01),
        "sgu_w": nrm(ks[11], (DEPTH, SGU_HEADS, CHUNK, CHUNK), CHUNK ** -0.5),
        "sgu_b": 1.0 + nrm(ks[12], (DEPTH, SGU_HEADS, CHUNK), 0.1),
        "w_out": nrm(ks[13], (DEPTH, MIX_WIDTH, D_MODEL), MIX_WIDTH ** -0.5),
        "norm_ffn": 1.0 + nrm(ks[14], (DEPTH, D_MODEL), 0.01),
        "w_router": nrm(ks[15], (DEPTH, D_MODEL, N_EXPERTS), D_MODEL ** -0.5),
        "b_router": nrm(ks[16], (DEPTH, N_EXPERTS), 0.01),
        "w_up": nrm(ks[17], (DEPTH, N_EXPERTS, D_MODEL, 2 * D_FF), D_MODEL ** -0.5),
        "b_up": nrm(ks[18], (DEPTH, N_EXPERTS, 2 * D_FF), 0.01),
        "w_down": nrm(ks[19], (DEPTH, N_EXPERTS, D_FF, D_MODEL), D_FF ** -0.5),
        "b_down": nrm(ks[20], (DEPTH, N_EXPERTS, D_MODEL), 0.01),
        "w_ple_gate": nrm(ks[21], (DEPTH, D_MODEL, D_MODEL), D_MODEL ** -0.5),
        "w_ple_proj": nrm(ks[22], (DEPTH, PLE_DIM, D_MODEL), PLE_DIM ** -0.5),
        "norm_final": 1.0 + nrm(ks[23], (D_MODEL,), 0.01),
    }


def reference(x_prompt, x_sample, state_pool, p_prompt, p_sample, norm_mix, w_in, w_pool,
              pool_scale, sgu_ln_g, sgu_ln_b, sgu_w, sgu_b, w_out, norm_ffn, w_router, b_router,
              w_up, b_up, w_down, b_down, w_ple_gate, w_ple_proj, norm_final):
    xp, xs = x_prompt, x_sample
    zero_prefix = jnp.zeros((x_prompt.shape[0], POOL_BUF, POOL_WIDTH), x_prompt.dtype)
    pool_p, pool_s, v_s = [], [], []
    for i in range(DEPTH):
        lw = (norm_mix[i], w_in[i], w_pool[i], pool_scale[i], sgu_ln_g[i], sgu_ln_b[i], sgu_w[i],
              sgu_b[i], w_out[i], norm_ffn[i], w_router[i], b_router[i], w_up[i], b_up[i],
              w_down[i], b_down[i], w_ple_gate[i], w_ple_proj[i])
        xp, bp, _ = layer(xp, p_prompt[i], zero_prefix, 0, *lw)
        xs, bs, vs = layer(xs, p_sample[i], state_pool[i], PAST_LEN, *lw)
        pool_p.append(bp)
        pool_s.append(bs)
        v_s.append(vs)
    y_prompt = rmsnorm(xp, norm_final)
    y_sample = rmsnorm(xs, norm_final)
    new_pool_prompt = jnp.stack(pool_p, axis=0)
    new_pool_sample = jnp.stack(pool_s, axis=0)
    new_sgu_v_sample = jnp.stack(v_s, axis=0)
    return (y_prompt, y_sample, new_pool_prompt, new_pool_sample, new_sgu_v_sample)
```

```python
import functools

import jax
import jax.numpy as jnp
from jax import lax
from jax.experimental import pallas as pl
from jax.experimental.pallas import tpu as pltpu

F32 = jnp.float32
BF16 = jnp.bfloat16
I32 = jnp.int32

POOL_WINDOWS = (2, 4, 8, 16)
POOL_BUF = max(POOL_WINDOWS) - 1
SGU_HEADS = 4
CHUNK = 128
TOP_K = 4
SWIGLU_LIMIT = 7.0
SWIGLU_ALPHA = 1.702
RMS_EPS = 1e-6
LN_EPS = 1e-5

SUBLANES = 8
CARRY_ROWS = 16
TS_MIX = 512
TS_SCATTER = 512
TM_MOE = 256
FF_CHUNK = 512
TS_COMBINE = 256
VMEM_LIMIT_BYTES = 56 * 1024 * 1024


def _rms(x, g):
    return x * lax.rsqrt(jnp.mean(x * x, axis=-1, keepdims=True) + RMS_EPS) * g


def _dot(a, b):
    return jnp.dot(a, b, preferred_element_type=F32)


def _gelu(x):
    return 0.5 * x * (1.0 + lax.erf(x * (2.0 ** -0.5)))


def _head_layernorm(v, g, b, hd):
    outs = []
    for h in range(SGU_HEADS):
        vh = v[:, h * hd:(h + 1) * hd]
        mu = jnp.mean(vh, axis=-1, keepdims=True)
        d = vh - mu
        var = jnp.mean(d * d, axis=-1, keepdims=True)
        outs.append(d * lax.rsqrt(var + LN_EPS) * g[:, h * hd:(h + 1) * hd] + b[:, h * hd:(h + 1) * hd])
    return outs


def _pool_project(diffs, wpool_ref, pscale):
    gd = diffs[0].shape[-1]
    outs = []
    for g, d in enumerate(diffs):
        outs.append(_dot(d.astype(BF16), wpool_ref[g]) * pscale[:, g * gd:(g + 1) * gd])
    return jnp.concatenate(outs, axis=-1).astype(BF16)


def _route_and_store(x, a_bf, b_bf, wout_ref, nf_ref, wrt_ref, br_ref, cnt_ref,
                     x1_ref, idx_ref, w_ref, rank_ref, cnt_out_ref):
    ts = x.shape[0]
    pw = a_bf.shape[-1]
    n_exp = wrt_ref.shape[0]
    mix = _dot(a_bf, wout_ref[:pw, :]) + _dot(b_bf, wout_ref[pw:, :])
    x1 = x + mix
    x1_ref[...] = x1
    h2 = _rms(x1, nf_ref[...])
    logits = lax.dot_general(wrt_ref[...], h2, (((1,), (1,)), ((), ())),
                             precision=lax.Precision.HIGHEST,
                             preferred_element_type=F32) + br_ref[...]
    eio = lax.broadcasted_iota(I32, (n_exp, ts), 0).astype(F32)
    l = logits
    tops, sels = [], []
    for _ in range(TOP_K):
        m = jnp.max(l, axis=0, keepdims=True)
        sel = jnp.min(jnp.where(l == m, eio, float(n_exp)), axis=0, keepdims=True)
        tops.append(m)
        sels.append(sel)
        l = jnp.where(eio == sel, -jnp.inf, l)
    exps = [jnp.exp(t - tops[0]) for t in tops]
    denom = exps[0] + exps[1] + exps[2] + exps[3]
    onehot = jnp.zeros((n_exp, ts), F32)
    for sel in sels:
        onehot = onehot + jnp.where(eio == sel, 1.0, 0.0)
    r_i = lax.broadcasted_iota(I32, (ts, ts), 0)
    c_i = lax.broadcasted_iota(I32, (ts, ts), 1)
    upper = jnp.where(r_i < c_i, 1.0, 0.0).astype(BF16)
    base = cnt_ref[:, 0:1]
    rank_all = _dot(onehot.astype(BF16), upper) + base
    for k in range(TOP_K):
        idx_ref[k:k + 1, :] = sels[k].astype(I32)
        w_ref[k:k + 1, :] = exps[k] / denom
        rk = jnp.sum(jnp.where(eio == sels[k], rank_all, 0.0), axis=0, keepdims=True)
        rank_ref[k:k + 1, :] = rk.astype(I32)
    cnt_ref[...] = cnt_ref[...] + jnp.sum(onehot, axis=1, keepdims=True)
    cnt_out_ref[...] = cnt_ref[...]


def _prompt_tile(t, x_ref, nm_ref, win_ref, wpool_ref, pscale_ref, lng_ref, lnb_ref, sw_ref, sbt_ref,
                 tail_ref, carry_ref, bo_ref, n_tiles, route):
    ts = x_ref.shape[0]
    pw = pscale_ref.shape[-1]
    gd = pw // len(POOL_WINDOWS)
    hd = pw // SGU_HEADS

    @pl.when(t == 0)
    def _():
        carry_ref[...] = jnp.zeros_like(carry_ref)

    x = x_ref[...]
    h = _rms(x, nm_ref[...])
    z = _dot(h.astype(BF16), win_ref[...])
    zp = z[:, :pw]

    ext = jnp.concatenate([carry_ref[...], zp], axis=0)
    carry_ref[...] = zp[ts - CARRY_ROWS:, :]

    @pl.when(t == n_tiles - 1)
    def _():
        tail_ref[0] = zp[ts - CARRY_ROWS:, :]

    pos = t * ts + lax.broadcasted_iota(I32, (ts, 1), 0)
    diffs = []
    for g, w in enumerate(POOL_WINDOWS):
        acc = ext[:, g * gd:(g + 1) * gd]
        k = 1
        while k < w:
            acc = acc + pltpu.roll(acc, k, 0)
            k *= 2
        cnt = jnp.minimum(w, pos + 1).astype(F32)
        diffs.append(acc[CARRY_ROWS:, :] / cnt - zp[:, g * gd:(g + 1) * gd])
    a_bf = _pool_project(diffs, wpool_ref, pscale_ref[...])

    zuv = _gelu(z[:, pw:])
    u = zuv[:, :pw]
    vn = _head_layernorm(zuv[:, pw:], lng_ref[...], lnb_ref[...], hd)
    ri = lax.broadcasted_iota(I32, (CHUNK, CHUNK), 0)
    ci = lax.broadcasted_iota(I32, (CHUNK, CHUNK), 1)
    for hh in range(SGU_HEADS):
        wm = jnp.where(ci <= ri, sw_ref[hh], 0.0).astype(BF16)
        bias = sbt_ref[:, hh:hh + 1]
        vh = vn[hh].astype(BF16)
        for c in range(ts // CHUNK):
            rows = slice(c * CHUNK, (c + 1) * CHUNK)
            mixed = _dot(wm, vh[rows, :]) + bias
            bo_ref[rows, hh * hd:(hh + 1) * hd] = (u[rows, hh * hd:(hh + 1) * hd] * mixed).astype(BF16)
    route(x, a_bf, bo_ref[...])


def _sample_tile(x_ref, st_ref, nm_ref, win_ref, wpool_ref, pscale_ref, lng_ref, lnb_ref, sw_ref,
                 sb_ref, zp_ref, vn_ref, route):
    n_new, nb, pw = zp_ref.shape
    gd = pw // len(POOL_WINDOWS)
    hd = pw // SGU_HEADS

    x = x_ref[...]
    h = _rms(x, nm_ref[...])
    z = _dot(h.astype(BF16), win_ref[...])
    zp = z[:, :pw]
    for l in range(n_new):
        zp_ref[l] = zp[l * nb:(l + 1) * nb, :]

    diffs = []
    for g, w in enumerate(POOL_WINDOWS):
        cols = slice(g * gd, (g + 1) * gd)
        lo_needed = POOL_BUF + 1 - w
        suffix = {}
        run = None
        for r in range(POOL_BUF - 1, lo_needed - 1, -1):
            slab = st_ref[r][:, cols]
            run = slab if run is None else run + slab
            suffix[r] = run
        pieces = []
        for l in range(n_new):
            first_hist = POOL_BUF + 1 + l - w
            tot = suffix[first_hist] if first_hist < POOL_BUF else None
            for l2 in range(max(0, l - w + 1), l + 1):
                zl = zp[l2 * nb:(l2 + 1) * nb, cols]
                tot = zl if tot is None else tot + zl
            pieces.append(tot / float(w) - zp[l * nb:(l + 1) * nb, cols])
        diffs.append(jnp.concatenate(pieces, axis=0))
    a_bf = _pool_project(diffs, wpool_ref, pscale_ref[...])

    zuv = _gelu(z[:, pw:])
    u = zuv[:, :pw]
    vn = _head_layernorm(zuv[:, pw:], lng_ref[...], lnb_ref[...], hd)
    vfull = jnp.concatenate(vn, axis=-1)
    for l in range(n_new):
        vn_ref[l] = vfull[l * nb:(l + 1) * nb, :]
    b_cols = []
    for hh in range(SGU_HEADS):
        pieces = []
        for l in range(n_new):
            mixed = None
            for l2 in range(l + 1):
                term = sw_ref[(hh * n_new + l) * n_new + l2] * vn[hh][l2 * nb:(l2 + 1) * nb, :]
                mixed = term if mixed is None else mixed + term
            mixed = mixed + sb_ref[hh * n_new + l]
            pieces.append(u[l * nb:(l + 1) * nb, hh * hd:(hh + 1) * hd] * mixed)
        b_cols.append(jnp.concatenate(pieces, axis=0))
    route(x, a_bf, jnp.concatenate(b_cols, axis=-1).astype(BF16))


def _mixer_body(xp_ref, xs_ref, st_ref, nm_ref, win_ref, wpool_ref, pscale_ref, lng_ref, lnb_ref,
                sw_ref, sbt_ref, sws_ref, sbs_ref, wout_ref, nf_ref, wrt_ref, br_ref,
                x1_ref, idx_ref, w_ref, rank_ref, cnt_out_ref, tail_ref, zp_ref, vn_ref,
                carry_ref, cnt_ref, bo_ref, *, tiles_per_prompt):
    i = pl.program_id(0)
    n_prompt_steps = pl.num_programs(0) - 1

    @pl.when(i == 0)
    def _():
        cnt_ref[...] = jnp.zeros_like(cnt_ref)

    route = functools.partial(_route_and_store, wout_ref=wout_ref, nf_ref=nf_ref, wrt_ref=wrt_ref,
                              br_ref=br_ref, cnt_ref=cnt_ref, x1_ref=x1_ref, idx_ref=idx_ref,
                              w_ref=w_ref, rank_ref=rank_ref, cnt_out_ref=cnt_out_ref)

    @pl.when(i < n_prompt_steps)
    def _():
        _prompt_tile(lax.rem(i, tiles_per_prompt), xp_ref, nm_ref, win_ref, wpool_ref, pscale_ref,
                     lng_ref, lnb_ref, sw_ref, sbt_ref, tail_ref, carry_ref, bo_ref,
                     tiles_per_prompt, route)

    @pl.when(i == n_prompt_steps)
    def _():
        _sample_tile(xs_ref, st_ref, nm_ref, win_ref, wpool_ref, pscale_ref, lng_ref, lnb_ref,
                     sws_ref, sbs_ref, zp_ref, vn_ref, route)


def _row_copy(src_hbm, src_row, dst, dst_row, sem):
    return pltpu.make_async_copy(src_hbm.at[pl.ds(src_row, 1)], dst.at[pl.ds(dst_row, 1)], sem)


def _scatter_body(pstart_ref, pcnt_ref, nused_ref, dest_ref, x1_hbm, xs_hbm, zero_ref, zsem, rsem):
    i = pl.program_id(0)
    ts = dest_ref.shape[1]
    tm = zero_ref.shape[0]
    n_exp = pstart_ref.shape[0]

    def aligned_end(e):
        return lax.div(pstart_ref[e] + pcnt_ref[e] + (SUBLANES - 1), SUBLANES) * SUBLANES

    def zero_block(e):
        off = pl.multiple_of(aligned_end(e), SUBLANES)
        return pltpu.make_async_copy(zero_ref, xs_hbm.at[pl.ds(off, tm)], zsem)

    def zero_row(e, j):
        return _row_copy(zero_ref, 0, xs_hbm, pstart_ref[e] + pcnt_ref[e] + j, zsem)

    def head_rows(e):
        return aligned_end(e) - (pstart_ref[e] + pcnt_ref[e])

    @pl.when(i == 0)
    def _():
        zero_ref[...] = jnp.zeros_like(zero_ref)
        for e in range(n_exp):
            zero_block(e).start()
            for j in range(SUBLANES - 1):
                @pl.when(j < head_rows(e))
                def _():
                    zero_row(e, j).start()
        for e in range(n_exp):
            zero_block(e).wait()
            for j in range(SUBLANES - 1):
                @pl.when(j < head_rows(e))
                def _():
                    zero_row(e, j).wait()

        def tail_block(b):
            return pltpu.make_async_copy(zero_ref, xs_hbm.at[pl.ds(pl.multiple_of(b * tm, tm), tm)], zsem)

        n_total = xs_hbm.shape[0] // tm

        def start_tail(b, c):
            tail_block(b).start()
            return c

        def wait_tail(b, c):
            tail_block(b).wait()
            return c

        lax.fori_loop(nused_ref[0], n_total, start_tail, 0)
        lax.fori_loop(nused_ref[0], n_total, wait_tail, 0)

    def start_rows(r, carry):
        for k in range(TOP_K):
            _row_copy(x1_hbm, i * ts + r, xs_hbm, dest_ref[k, r], rsem).start()
        return carry

    lax.fori_loop(0, ts, start_rows, 0, unroll=8)

    def wait_rows(r, carry):
        for k in range(TOP_K):
            _row_copy(x1_hbm, 0, xs_hbm, 0, rsem).wait()
        return carry

    lax.fori_loop(0, ts, wait_rows, 0, unroll=8)


def _experts_body(be_ref, nused_ref, xs_ref, nf_ref, wup_ref, bup_ref, wdn_ref, bdn_ref, ys_ref,
                  wup_bf, wdn_bf):
    b = pl.program_id(0)
    d_ff = wdn_ref.shape[1]

    @pl.when(b < nused_ref[0])
    def _():
        prev = be_ref[jnp.maximum(b - 1, 0)]

        @pl.when((b == 0) | (be_ref[b] != prev))
        def _():
            rows = 128

            def cast_up(j, c):
                sl = pl.ds(pl.multiple_of(j * rows, rows), rows)
                wup_bf[sl, :] = wup_ref[0, sl, :].astype(BF16)
                return c

            lax.fori_loop(0, wup_ref.shape[1] // rows, cast_up, 0)

            def cast_dn(j, c):
                sl = pl.ds(pl.multiple_of(j * rows, rows), rows)
                wdn_bf[sl, :] = wdn_ref[0, sl, :].astype(BF16)
                return c

            lax.fori_loop(0, wdn_ref.shape[1] // rows, cast_dn, 0)

        xb = _rms(xs_ref[...], nf_ref[...]).astype(BF16)
        acc = None
        for j in range(0, d_ff, FF_CHUNK):
            glu = _dot(xb, wup_bf[:, j:j + FF_CHUNK]) + bup_ref[0, :, j:j + FF_CHUNK]
            lin = _dot(xb, wup_bf[:, d_ff + j:d_ff + j + FF_CHUNK]) + bup_ref[0, :, d_ff + j:d_ff + j + FF_CHUNK]
            glu = jnp.minimum(glu, SWIGLU_LIMIT)
            lin = jnp.clip(lin, -SWIGLU_LIMIT, SWIGLU_LIMIT)
            act = glu * jax.nn.sigmoid(SWIGLU_ALPHA * glu) * (lin + 1.0)
            part = _dot(act.astype(BF16), wdn_bf[j:j + FF_CHUNK, :])
            acc = part if acc is None else acc + part
        ys_ref[...] = acc + bdn_ref[0]

    @pl.when(b >= nused_ref[0])
    def _():
        ys_ref[...] = jnp.zeros_like(ys_ref)


def _combine_body(dcur_ref, dnext_ref, ys_hbm, x1_ref, wt_ref, p_ref, wg_ref, wp_ref, nfin_ref,
                  yp_ref, ysm_ref, buf, sem, *, n_prompt_tiles):
    i = pl.program_id(0)
    n = pl.num_programs(0)
    ts = x1_ref.shape[0]
    slot = i % 2

    def issue(dref, sl):
        def body(r, c):
            for k in range(TOP_K):
                _row_copy(ys_hbm, dref[k, r], buf.at[sl, k], r, sem.at[sl]).start()
            return c
        lax.fori_loop(0, ts, body, 0, unroll=8)

    @pl.when(i == 0)
    def _():
        issue(dcur_ref, 0)

    @pl.when(i + 1 < n)
    def _():
        issue(dnext_ref, 1 - slot)

    def wait_rows(r, c):
        for k in range(TOP_K):
            _row_copy(ys_hbm, 0, buf.at[slot, k], 0, sem.at[slot]).wait()
        return c

    lax.fori_loop(0, ts, wait_rows, 0, unroll=8)

    acc = x1_ref[...]
    wt = wt_ref[...]
    for k in range(TOP_K):
        acc = acc + buf[slot, k] * wt[:, k:k + 1]
    gate = jax.nn.sigmoid(_dot(acc.astype(BF16), wg_ref[...]))
    proj = _dot(p_ref[...].astype(BF16), wp_ref[...])
    y = _rms(acc + gate * proj, nfin_ref[...])

    @pl.when(i < n_prompt_tiles)
    def _():
        yp_ref[...] = y

    @pl.when(i >= n_prompt_tiles)
    def _():
        ysm_ref[...] = y


def _const_spec(shape):
    nd = len(shape)
    return pl.BlockSpec(shape, lambda *_: (0,) * nd)


def kernel(x_prompt, x_sample, state_pool, p_prompt, p_sample, norm_mix, w_in, w_pool, pool_scale,
           sgu_ln_g, sgu_ln_b, sgu_w, sgu_b, w_out, norm_ffn, w_router, b_router, w_up, b_up,
           w_down, b_down, w_ple_gate, w_ple_proj, norm_final):
    B, L, D = x_prompt.shape
    NB, NL, _ = x_sample.shape
    PW = state_pool.shape[-1]
    NE = w_router.shape[-1]
    DFF = w_down.shape[2]
    PLE = p_prompt.shape[-1]
    TP = B * L
    TSM = NB * NL
    T = TP + TSM
    assert w_in.shape[0] == 1, "single trunk layer"
    assert L % TS_MIX == 0 and TSM == TS_MIX and TS_MIX % CHUNK == 0
    assert T % TS_SCATTER == 0 and TP % TS_COMBINE == 0 and TSM % TS_COMBINE == 0
    assert state_pool.shape[2] == POOL_BUF and PW % len(POOL_WINDOWS) == 0 and PW % SGU_HEADS == 0

    nm = norm_mix[0].reshape(1, D)
    nf = norm_ffn[0].reshape(1, D)
    nfin = norm_final.reshape(1, D)
    win_bf = w_in[0].astype(BF16)
    wpool_bf = w_pool[0].astype(BF16)
    pscale = pool_scale[0].reshape(1, PW)
    lng = sgu_ln_g[0].reshape(1, PW)
    lnb = sgu_ln_b[0].reshape(1, PW)
    wout_bf = w_out[0].astype(BF16)
    wrt = w_router[0].T
    br = b_router[0].reshape(NE, 1)
    wg_bf = w_ple_gate[0].astype(BF16)
    wp_bf = w_ple_proj[0].astype(BF16)

    cparams = functools.partial(pltpu.CompilerParams, vmem_limit_bytes=VMEM_LIMIT_BYTES)
    NT = L // TS_MIX
    n_prompt_steps = B * NT
    xs_rows = x_sample.transpose(1, 0, 2).reshape(TSM, D)
    hist = state_pool[0].transpose(1, 0, 2)
    sw_small = sgu_w[0, :, :NL, :NL].reshape(-1)
    sb_small = sgu_b[0, :, :NL].reshape(-1)
    smem = pl.BlockSpec(memory_space=pltpu.SMEM)
    any_spec = pl.BlockSpec(memory_space=pl.ANY)
    last_prompt = n_prompt_steps - 1
    x1_all, idx_t, w_t, rank_t, counts, pool_tail, zp_s, vn_s = pl.pallas_call(
        functools.partial(_mixer_body, tiles_per_prompt=NT),
        grid=(n_prompt_steps + 1,),
        in_specs=[
            pl.BlockSpec((TS_MIX, D), lambda i: (jnp.minimum(i, last_prompt), 0)),
            _const_spec((TSM, D)), _const_spec(hist.shape),
            _const_spec((1, D)), _const_spec(win_bf.shape), _const_spec(wpool_bf.shape),
            _const_spec((1, PW)), _const_spec((1, PW)), _const_spec((1, PW)),
            _const_spec(sgu_w[0].shape), _const_spec((CHUNK, SGU_HEADS)), smem, smem,
            _const_spec(wout_bf.shape), _const_spec((1, D)), _const_spec(wrt.shape),
            _const_spec((NE, 1)),
        ],
        out_specs=[
            pl.BlockSpec((TS_MIX, D), lambda i: (i, 0)),
            pl.BlockSpec((TOP_K, TS_MIX), lambda i: (0, i)),
            pl.BlockSpec((TOP_K, TS_MIX), lambda i: (0, i)),
            pl.BlockSpec((TOP_K, TS_MIX), lambda i: (0, i)),
            _const_spec((NE, 128)),
            pl.BlockSpec((1, CARRY_ROWS, PW), lambda i: (jnp.minimum(i, last_prompt) // NT, 0, 0)),
            _const_spec((NL, NB, PW)),
            _const_spec((NL, NB, PW)),
        ],
        out_shape=[
            jax.ShapeDtypeStruct((T, D), F32),
            jax.ShapeDtypeStruct((TOP_K, T), I32),
            jax.ShapeDtypeStruct((TOP_K, T), F32),
            jax.ShapeDtypeStruct((TOP_K, T), I32),
            jax.ShapeDtypeStruct((NE, 128), F32),
            jax.ShapeDtypeStruct((B, CARRY_ROWS, PW), F32),
            jax.ShapeDtypeStruct((NL, NB, PW), F32),
            jax.ShapeDtypeStruct((NL, NB, PW), F32),
        ],
        scratch_shapes=[pltpu.VMEM((CARRY_ROWS, PW), F32), pltpu.VMEM((NE, 128), F32),
                        pltpu.VMEM((TS_MIX, PW), BF16)],
        compiler_params=cparams(dimension_semantics=("arbitrary",)),
        name="mixer",
    )(x_prompt.reshape(TP, D), xs_rows, hist, nm, win_bf, wpool_bf, pscale, lng, lnb, sgu_w[0],
      sgu_b[0].T, sw_small, sb_small, wout_bf, nf, wrt, br)

    A = T * TOP_K
    n_blocks = -(-A // TM_MOE) + NE
    n_slots = n_blocks * TM_MOE
    cnt_i = counts[:, 0].astype(I32)
    padded = (cnt_i + TM_MOE - 1) // TM_MOE * TM_MOE
    pad_end = jnp.cumsum(padded)
    pad_start = pad_end - padded
    n_used = (pad_end[-1] // TM_MOE).astype(I32).reshape(1)
    blk = jnp.minimum(jnp.arange(n_blocks, dtype=I32), n_used[0] - 1)
    block_e = jnp.minimum(jnp.searchsorted(pad_end, blk * TM_MOE, side='right'), NE - 1).astype(I32)
    dest = pad_start[idx_t] + rank_t

    xs_sorted = pl.pallas_call(
        _scatter_body,
        grid_spec=pltpu.PrefetchScalarGridSpec(
            num_scalar_prefetch=3,
            grid=(T // TS_SCATTER,),
            in_specs=[pl.BlockSpec((TOP_K, TS_SCATTER), lambda i, *_: (0, i), memory_space=pltpu.SMEM),
                      any_spec],
            out_specs=any_spec,
            scratch_shapes=[pltpu.VMEM((TM_MOE, D), F32), pltpu.SemaphoreType.DMA(()),
                            pltpu.SemaphoreType.DMA(())],
        ),
        out_shape=jax.ShapeDtypeStruct((n_slots + TM_MOE, D), F32),
        compiler_params=cparams(dimension_semantics=("arbitrary",)),
        name="moe_scatter",
    )(pad_start, cnt_i, n_used, dest, x1_all)

    ys_sorted = pl.pallas_call(
        _experts_body,
        grid_spec=pltpu.PrefetchScalarGridSpec(
            num_scalar_prefetch=2,
            grid=(n_blocks,),
            in_specs=[
                pl.BlockSpec((TM_MOE, D), lambda b, be, nu: (jnp.minimum(b, nu[0] - 1), 0)),
                pl.BlockSpec((1, D), lambda b, be, nu: (0, 0)),
                pl.BlockSpec((1, D, 2 * DFF), lambda b, be, nu: (be[b], 0, 0)),
                pl.BlockSpec((1, 1, 2 * DFF), lambda b, be, nu: (be[b], 0, 0)),
                pl.BlockSpec((1, DFF, D), lambda b, be, nu: (be[b], 0, 0)),
                pl.BlockSpec((1, 1, D), lambda b, be, nu: (be[b], 0, 0)),
            ],
            out_specs=pl.BlockSpec((TM_MOE, D), lambda b, be, nu: (b, 0)),
            scratch_shapes=[pltpu.VMEM((D, 2 * DFF), BF16), pltpu.VMEM((DFF, D), BF16)],
        ),
        out_shape=jax.ShapeDtypeStruct((n_slots, D), F32),
        compiler_params=cparams(dimension_semantics=("arbitrary",)),
        name="moe_experts",
    )(block_e, n_used, xs_sorted, nf, w_up[0], b_up[0].reshape(NE, 1, 2 * DFF), w_down[0],
      b_down[0].reshape(NE, 1, D))

    p_rows = jnp.concatenate([p_prompt[0].reshape(TP, PLE),
                              p_sample[0].transpose(1, 0, 2).reshape(TSM, PLE)], axis=0)
    n_tiles = T // TS_COMBINE
    npt = TP // TS_COMBINE
    y_p, y_s = pl.pallas_call(
        functools.partial(_combine_body, n_prompt_tiles=npt),
        grid=(n_tiles,),
        in_specs=[
            pl.BlockSpec((TOP_K, TS_COMBINE), lambda i: (0, i), memory_space=pltpu.SMEM),
            pl.BlockSpec((TOP_K, TS_COMBINE), lambda i: (0, jnp.minimum(i + 1, n_tiles - 1)),
                         memory_space=pltpu.SMEM),
            any_spec,
            pl.BlockSpec((TS_COMBINE, D), lambda i: (i, 0)),
            pl.BlockSpec((TS_COMBINE, TOP_K), lambda i: (i, 0)),
            pl.BlockSpec((TS_COMBINE, PLE), lambda i: (i, 0)),
            _const_spec(wg_bf.shape), _const_spec(wp_bf.shape), _const_spec((1, D)),
        ],
        out_specs=[
            pl.BlockSpec((TS_COMBINE, D), lambda i: (jnp.minimum(i, npt - 1), 0)),
            pl.BlockSpec((TS_COMBINE, D), lambda i: (jnp.maximum(i - npt, 0), 0)),
        ],
        out_shape=[jax.ShapeDtypeStruct((TP, D), F32), jax.ShapeDtypeStruct((TSM, D), F32)],
        scratch_shapes=[pltpu.VMEM((2, TOP_K, TS_COMBINE, D), F32), pltpu.SemaphoreType.DMA((2,))],
        compiler_params=cparams(dimension_semantics=("arbitrary",)),
        name="moe_combine",
    )(dest, dest, ys_sorted, x1_all, w_t.T, p_rows, wg_bf, wp_bf, nfin)

    y_prompt = y_p.reshape(B, L, D)
    y_sample = y_s.reshape(NL, NB, D).transpose(1, 0, 2)
    new_pool_prompt = pool_tail[:, CARRY_ROWS - POOL_BUF:, :][None]
    new_pool_sample = jnp.concatenate([state_pool[0][:, NL:, :], zp_s.transpose(1, 0, 2)], axis=1)[None]
    new_sgu_v_sample = vn_s.transpose(1, 0, 2)[None]
    return (y_prompt, y_sample, new_pool_prompt, new_pool_sample, new_sgu_v_sample)
```

```python
import functools

import jax
import jax.numpy as jnp
from jax import lax
from jax.experimental import pallas as pl
from jax.experimental.pallas import tpu as pltpu

F32 = jnp.float32
BF16 = jnp.bfloat16
I32 = jnp.int32

POOL_WINDOWS = (2, 4, 8, 16)
POOL_BUF = max(POOL_WINDOWS) - 1
SGU_HEADS = 4
CHUNK = 128
TOP_K = 4
SWIGLU_LIMIT = 7.0
SWIGLU_ALPHA = 1.702
RMS_EPS = 1e-6
LN_EPS = 1e-5

SUBLANES = 8
LANES = 128
CARRY_ROWS = 16
TS_MIX = 512
TS_SCATTER = 512
TM_MOE = 256
FF_CHUNK = 512
TS_COMBINE = 256
VMEM_LIMIT_BYTES = 56 * 1024 * 1024


def _rms(x, g):
    return x * lax.rsqrt(jnp.mean(x * x, axis=-1, keepdims=True) + RMS_EPS) * g


def _dot(a, b):
    return jnp.dot(a, b, preferred_element_type=F32)


def _load_rows(ref, n_rows):
    return jnp.concatenate([ref[pl.ds(c, n_rows, stride=SUBLANES), :] for c in range(SUBLANES)], axis=-1)


def _store_rows(ref, val):
    n_rows = val.shape[0]
    for c in range(SUBLANES):
        ref[pl.ds(c, n_rows, stride=SUBLANES), :] = val[:, c * LANES:(c + 1) * LANES]


def _gelu(x):
    return 0.5 * x * (1.0 + lax.erf(x * (2.0 ** -0.5)))


def _head_layernorm(v, g, b, hd):
    outs = []
    for h in range(SGU_HEADS):
        vh = v[:, h * hd:(h + 1) * hd]
        mu = jnp.mean(vh, axis=-1, keepdims=True)
        d = vh - mu
        var = jnp.mean(d * d, axis=-1, keepdims=True)
        outs.append(d * lax.rsqrt(var + LN_EPS) * g[:, h * hd:(h + 1) * hd] + b[:, h * hd:(h + 1) * hd])
    return outs


def _pool_project(diffs, wpool_ref, pscale):
    gd = diffs[0].shape[-1]
    outs = []
    for g, d in enumerate(diffs):
        outs.append(_dot(d.astype(BF16), wpool_ref[g]) * pscale[:, g * gd:(g + 1) * gd])
    return jnp.concatenate(outs, axis=-1).astype(BF16)


def _route_and_store(x, a_bf, b_bf, wout_ref, nf_ref, wrt_ref, br_ref, cnt_ref,
                     x1_ref, idx_ref, w_ref, rank_ref, cnt_out_ref):
    ts = x.shape[0]
    pw = a_bf.shape[-1]
    n_exp = wrt_ref.shape[0]
    mix = _dot(a_bf, wout_ref[:pw, :]) + _dot(b_bf, wout_ref[pw:, :])
    x1 = x + mix
    _store_rows(x1_ref, x1)
    h2 = _rms(x1, nf_ref[...])
    logits = lax.dot_general(wrt_ref[...], h2, (((1,), (1,)), ((), ())),
                             precision=lax.Precision.HIGHEST,
                             preferred_element_type=F32) + br_ref[...]
    eio = lax.broadcasted_iota(I32, (n_exp, ts), 0).astype(F32)
    l = logits
    tops, sels = [], []
    for _ in range(TOP_K):
        m = jnp.max(l, axis=0, keepdims=True)
        sel = jnp.min(jnp.where(l == m, eio, float(n_exp)), axis=0, keepdims=True)
        tops.append(m)
        sels.append(sel)
        l = jnp.where(eio == sel, -jnp.inf, l)
    exps = [jnp.exp(t - tops[0]) for t in tops]
    denom = exps[0] + exps[1] + exps[2] + exps[3]
    onehot = jnp.zeros((n_exp, ts), F32)
    for sel in sels:
        onehot = onehot + jnp.where(eio == sel, 1.0, 0.0)
    r_i = lax.broadcasted_iota(I32, (ts, ts), 0)
    c_i = lax.broadcasted_iota(I32, (ts, ts), 1)
    upper = jnp.where(r_i < c_i, 1.0, 0.0).astype(BF16)
    base = cnt_ref[:, 0:1]
    rank_all = _dot(onehot.astype(BF16), upper) + base
    for k in range(TOP_K):
        idx_ref[k:k + 1, :] = sels[k].astype(I32)
        w_ref[k:k + 1, :] = exps[k] / denom
        rk = jnp.sum(jnp.where(eio == sels[k], rank_all, 0.0), axis=0, keepdims=True)
        rank_ref[k:k + 1, :] = rk.astype(I32)
    cnt_ref[...] = cnt_ref[...] + jnp.sum(onehot, axis=1, keepdims=True)
    cnt_out_ref[...] = cnt_ref[...]


def _prompt_tile(t, x_ref, nm_ref, win_ref, wpool_ref, pscale_ref, lng_ref, lnb_ref, sw_ref, sbt_ref,
                 tail_ref, carry_ref, bo_ref, n_tiles, route):
    ts = x_ref.shape[0]
    pw = pscale_ref.shape[-1]
    gd = pw // len(POOL_WINDOWS)
    hd = pw // SGU_HEADS

    @pl.when(t == 0)
    def _():
        carry_ref[...] = jnp.zeros_like(carry_ref)

    x = x_ref[...]
    h = _rms(x, nm_ref[...])
    z = _dot(h.astype(BF16), win_ref[...])
    zp = z[:, :pw]

    ext = jnp.concatenate([carry_ref[...], zp], axis=0)
    carry_ref[...] = zp[ts - CARRY_ROWS:, :]

    @pl.when(t == n_tiles - 1)
    def _():
        tail_ref[0] = zp[ts - CARRY_ROWS:, :]

    pos = t * ts + lax.broadcasted_iota(I32, (ts, 1), 0)
    diffs = []
    for g, w in enumerate(POOL_WINDOWS):
        acc = ext[:, g * gd:(g + 1) * gd]
        k = 1
        while k < w:
            acc = acc + pltpu.roll(acc, k, 0)
            k *= 2
        cnt = jnp.minimum(w, pos + 1).astype(F32)
        diffs.append(acc[CARRY_ROWS:, :] / cnt - zp[:, g * gd:(g + 1) * gd])
    a_bf = _pool_project(diffs, wpool_ref, pscale_ref[...])

    zuv = _gelu(z[:, pw:])
    u = zuv[:, :pw]
    vn = _head_layernorm(zuv[:, pw:], lng_ref[...], lnb_ref[...], hd)
    ri = lax.broadcasted_iota(I32, (CHUNK, CHUNK), 0)
    ci = lax.broadcasted_iota(I32, (CHUNK, CHUNK), 1)
    for hh in range(SGU_HEADS):
        wm = jnp.where(ci <= ri, sw_ref[hh], 0.0).astype(BF16)
        bias = sbt_ref[:, hh:hh + 1]
        vh = vn[hh].astype(BF16)
        for c in range(ts // CHUNK):
            rows = slice(c * CHUNK, (c + 1) * CHUNK)
            mixed = _dot(wm, vh[rows, :]) + bias
            bo_ref[rows, hh * hd:(hh + 1) * hd] = (u[rows, hh * hd:(hh + 1) * hd] * mixed).astype(BF16)
    route(x, a_bf, bo_ref[...])


def _sample_tile(x_ref, st_ref, nm_ref, win_ref, wpool_ref, pscale_ref, lng_ref, lnb_ref, sw_ref,
                 sb_ref, zp_ref, vn_ref, route):
    n_new, nb, pw = zp_ref.shape
    gd = pw // len(POOL_WINDOWS)
    hd = pw // SGU_HEADS

    x = x_ref[...]
    h = _rms(x, nm_ref[...])
    z = _dot(h.astype(BF16), win_ref[...])
    zp = z[:, :pw]
    for l in range(n_new):
        zp_ref[l] = zp[l * nb:(l + 1) * nb, :]

    diffs = []
    for g, w in enumerate(POOL_WINDOWS):
        cols = slice(g * gd, (g + 1) * gd)
        lo_needed = POOL_BUF + 1 - w
        suffix = {}
        run = None
        for r in range(POOL_BUF - 1, lo_needed - 1, -1):
            slab = st_ref[r][:, cols]
            run = slab if run is None else run + slab
            suffix[r] = run
        pieces = []
        for l in range(n_new):
            first_hist = POOL_BUF + 1 + l - w
            tot = suffix[first_hist] if first_hist < POOL_BUF else None
            for l2 in range(max(0, l - w + 1), l + 1):
                zl = zp[l2 * nb:(l2 + 1) * nb, cols]
                tot = zl if tot is None else tot + zl
            pieces.append(tot / float(w) - zp[l * nb:(l + 1) * nb, cols])
        diffs.append(jnp.concatenate(pieces, axis=0))
    a_bf = _pool_project(diffs, wpool_ref, pscale_ref[...])

    zuv = _gelu(z[:, pw:])
    u = zuv[:, :pw]
    vn = _head_layernorm(zuv[:, pw:], lng_ref[...], lnb_ref[...], hd)
    vfull = jnp.concatenate(vn, axis=-1)
    for l in range(n_new):
        vn_ref[l] = vfull[l * nb:(l + 1) * nb, :]
    b_cols = []
    for hh in range(SGU_HEADS):
        pieces = []
        for l in range(n_new):
            mixed = None
            for l2 in range(l + 1):
                term = sw_ref[(hh * n_new + l) * n_new + l2] * vn[hh][l2 * nb:(l2 + 1) * nb, :]
                mixed = term if mixed is None else mixed + term
            mixed = mixed + sb_ref[hh * n_new + l]
            pieces.append(u[l * nb:(l + 1) * nb, hh * hd:(hh + 1) * hd] * mixed)
        b_cols.append(jnp.concatenate(pieces, axis=0))
    route(x, a_bf, jnp.concatenate(b_cols, axis=-1).astype(BF16))


def _mixer_body(xp_ref, xs_ref, st_ref, nm_ref, win_ref, wpool_ref, pscale_ref, lng_ref, lnb_ref,
                sw_ref, sbt_ref, sws_ref, sbs_ref, wout_ref, nf_ref, wrt_ref, br_ref,
                x1_ref, idx_ref, w_ref, rank_ref, cnt_out_ref, tail_ref, zp_ref, vn_ref,
                carry_ref, cnt_ref, bo_ref, *, tiles_per_prompt):
    i = pl.program_id(0)
    n_prompt_steps = pl.num_programs(0) - 1

    @pl.when(i == 0)
    def _():
        cnt_ref[...] = jnp.zeros_like(cnt_ref)

    route = functools.partial(_route_and_store, wout_ref=wout_ref, nf_ref=nf_ref, wrt_ref=wrt_ref,
                              br_ref=br_ref, cnt_ref=cnt_ref, x1_ref=x1_ref, idx_ref=idx_ref,
                              w_ref=w_ref, rank_ref=rank_ref, cnt_out_ref=cnt_out_ref)

    @pl.when(i < n_prompt_steps)
    def _():
        _prompt_tile(lax.rem(i, tiles_per_prompt), xp_ref, nm_ref, win_ref, wpool_ref, pscale_ref,
                     lng_ref, lnb_ref, sw_ref, sbt_ref, tail_ref, carry_ref, bo_ref,
                     tiles_per_prompt, route)

    @pl.when(i == n_prompt_steps)
    def _():
        _sample_tile(xs_ref, st_ref, nm_ref, win_ref, wpool_ref, pscale_ref, lng_ref, lnb_ref,
                     sws_ref, sbs_ref, zp_ref, vn_ref, route)


def _row_copy(src, src_row, dst, dst_row, sem):
    s_rows = pl.ds(pl.multiple_of(src_row * SUBLANES, SUBLANES), SUBLANES)
    d_rows = pl.ds(pl.multiple_of(dst_row * SUBLANES, SUBLANES), SUBLANES)
    return pltpu.make_async_copy(src.at[s_rows], dst.at[d_rows], sem)


def _scatter_body(pstart_ref, pcnt_ref, nused_ref, dest_ref, x1_ref, xs_hbm, zero_ref, zsem, rsem):
    i = pl.program_id(0)
    ts = dest_ref.shape[1]
    tm = zero_ref.shape[0] // SUBLANES
    n_exp = pstart_ref.shape[0]

    def zero_block(row):
        rows = pl.ds(pl.multiple_of(row * SUBLANES, SUBLANES), tm * SUBLANES)
        return pltpu.make_async_copy(zero_ref, xs_hbm.at[rows], zsem)

    @pl.when(i == 0)
    def _():
        zero_ref[...] = jnp.zeros_like(zero_ref)
        for e in range(n_exp):
            zero_block(pstart_ref[e] + pcnt_ref[e]).start()
        for e in range(n_exp):
            zero_block(pstart_ref[e] + pcnt_ref[e]).wait()

        n_total = xs_hbm.shape[0] // (tm * SUBLANES)

        def start_tail(b, c):
            zero_block(b * tm).start()
            return c

        def wait_tail(b, c):
            zero_block(b * tm).wait()
            return c

        lax.fori_loop(nused_ref[0], n_total, start_tail, 0)
        lax.fori_loop(nused_ref[0], n_total, wait_tail, 0)

    def start_rows(r, carry):
        for k in range(TOP_K):
            _row_copy(x1_ref, r, xs_hbm, dest_ref[k, r], rsem).start()
        return carry

    lax.fori_loop(0, ts, start_rows, 0, unroll=8)

    def wait_rows(r, carry):
        for k in range(TOP_K):
            _row_copy(x1_ref, 0, xs_hbm, 0, rsem).wait()
        return carry

    lax.fori_loop(0, ts, wait_rows, 0, unroll=8)


def _experts_body(be_ref, nused_ref, xs_ref, nf_ref, wup_ref, bup_ref, wdn_ref, bdn_ref, ys_ref,
                  wup_bf, wdn_bf):
    b = pl.program_id(0)
    d_ff = wdn_ref.shape[1]

    @pl.when(b < nused_ref[0])
    def _():
        prev = be_ref[jnp.maximum(b - 1, 0)]

        @pl.when((b == 0) | (be_ref[b] != prev))
        def _():
            rows = 128

            def cast_up(j, c):
                sl = pl.ds(pl.multiple_of(j * rows, rows), rows)
                wup_bf[sl, :] = wup_ref[0, sl, :].astype(BF16)
                return c

            lax.fori_loop(0, wup_ref.shape[1] // rows, cast_up, 0)

            def cast_dn(j, c):
                sl = pl.ds(pl.multiple_of(j * rows, rows), rows)
                wdn_bf[sl, :] = wdn_ref[0, sl, :].astype(BF16)
                return c

            lax.fori_loop(0, wdn_ref.shape[1] // rows, cast_dn, 0)

        xb = _rms(_load_rows(xs_ref, ys_ref.shape[0] // SUBLANES), nf_ref[...]).astype(BF16)
        acc = None
        for j in range(0, d_ff, FF_CHUNK):
            glu = _dot(xb, wup_bf[:, j:j + FF_CHUNK]) + bup_ref[0, :, j:j + FF_CHUNK]
            lin = _dot(xb, wup_bf[:, d_ff + j:d_ff + j + FF_CHUNK]) + bup_ref[0, :, d_ff + j:d_ff + j + FF_CHUNK]
            glu = jnp.minimum(glu, SWIGLU_LIMIT)
            lin = jnp.clip(lin, -SWIGLU_LIMIT, SWIGLU_LIMIT)
            act = glu * jax.nn.sigmoid(SWIGLU_ALPHA * glu) * (lin + 1.0)
            part = _dot(act.astype(BF16), wdn_bf[j:j + FF_CHUNK, :])
            acc = part if acc is None else acc + part
        _store_rows(ys_ref, acc + bdn_ref[0])

    @pl.when(b >= nused_ref[0])
    def _():
        ys_ref[...] = jnp.zeros_like(ys_ref)


def _combine_body(dcur_ref, dnext_ref, ys_hbm, x1_ref, wt_ref, p_ref, wg_ref, wp_ref, nfin_ref,
                  yp_ref, ysm_ref, buf, sem, *, n_prompt_tiles):
    i = pl.program_id(0)
    n = pl.num_programs(0)
    ts = x1_ref.shape[0] // SUBLANES
    slot = i % 2

    def issue(dref, sl):
        def body(r, c):
            for k in range(TOP_K):
                _row_copy(ys_hbm, dref[k, r], buf.at[sl, k], r, sem.at[sl]).start()
            return c
        lax.fori_loop(0, ts, body, 0, unroll=8)

    @pl.when(i == 0)
    def _():
        issue(dcur_ref, 0)

    @pl.when(i + 1 < n)
    def _():
        issue(dnext_ref, 1 - slot)

    def wait_rows(r, c):
        for k in range(TOP_K):
            _row_copy(ys_hbm, 0, buf.at[slot, k], 0, sem.at[slot]).wait()
        return c

    lax.fori_loop(0, ts, wait_rows, 0, unroll=8)

    acc = _load_rows(x1_ref, ts)
    wt = wt_ref[...]
    for k in range(TOP_K):
        acc = acc + _load_rows(buf.at[slot, k], ts) * wt[:, k:k + 1]
    gate = jax.nn.sigmoid(_dot(acc.astype(BF16), wg_ref[...]))
    proj = _dot(p_ref[...].astype(BF16), wp_ref[...])
    y = _rms(acc + gate * proj, nfin_ref[...])

    @pl.when(i < n_prompt_tiles)
    def _():
        yp_ref[...] = y

    @pl.when(i >= n_prompt_tiles)
    def _():
        ysm_ref[...] = y


def _const_spec(shape):
    nd = len(shape)
    return pl.BlockSpec(shape, lambda *_: (0,) * nd)


def kernel(x_prompt, x_sample, state_pool, p_prompt, p_sample, norm_mix, w_in, w_pool, pool_scale,
           sgu_ln_g, sgu_ln_b, sgu_w, sgu_b, w_out, norm_ffn, w_router, b_router, w_up, b_up,
           w_down, b_down, w_ple_gate, w_ple_proj, norm_final):
    B, L, D = x_prompt.shape
    NB, NL, _ = x_sample.shape
    PW = state_pool.shape[-1]
    NE = w_router.shape[-1]
    DFF = w_down.shape[2]
    PLE = p_prompt.shape[-1]
    TP = B * L
    TSM = NB * NL
    T = TP + TSM
    assert w_in.shape[0] == 1, "single trunk layer"
    assert L % TS_MIX == 0 and TSM == TS_MIX and TS_MIX % CHUNK == 0
    assert T % TS_SCATTER == 0 and TP % TS_COMBINE == 0 and TSM % TS_COMBINE == 0
    assert D == SUBLANES * LANES, "a token row must fill exactly one vreg tile"
    assert state_pool.shape[2] == POOL_BUF and PW % len(POOL_WINDOWS) == 0 and PW % SGU_HEADS == 0

    nm = norm_mix[0].reshape(1, D)
    nf = norm_ffn[0].reshape(1, D)
    nfin = norm_final.reshape(1, D)
    win_bf = w_in[0].astype(BF16)
    wpool_bf = w_pool[0].astype(BF16)
    pscale = pool_scale[0].reshape(1, PW)
    lng = sgu_ln_g[0].reshape(1, PW)
    lnb = sgu_ln_b[0].reshape(1, PW)
    wout_bf = w_out[0].astype(BF16)
    wrt = w_router[0].T
    br = b_router[0].reshape(NE, 1)
    wg_bf = w_ple_gate[0].astype(BF16)
    wp_bf = w_ple_proj[0].astype(BF16)

    cparams = functools.partial(pltpu.CompilerParams, vmem_limit_bytes=VMEM_LIMIT_BYTES)
    NT = L // TS_MIX
    n_prompt_steps = B * NT
    xs_rows = x_sample.transpose(1, 0, 2).reshape(TSM, D)
    hist = state_pool[0].transpose(1, 0, 2)
    sw_small = sgu_w[0, :, :NL, :NL].reshape(-1)
    sb_small = sgu_b[0, :, :NL].reshape(-1)
    smem = pl.BlockSpec(memory_space=pltpu.SMEM)
    any_spec = pl.BlockSpec(memory_space=pl.ANY)
    last_prompt = n_prompt_steps - 1
    x1_all, idx_t, w_t, rank_t, counts, pool_tail, zp_s, vn_s = pl.pallas_call(
        functools.partial(_mixer_body, tiles_per_prompt=NT),
        grid=(n_prompt_steps + 1,),
        in_specs=[
            pl.BlockSpec((TS_MIX, D), lambda i: (jnp.minimum(i, last_prompt), 0)),
            _const_spec((TSM, D)), _const_spec(hist.shape),
            _const_spec((1, D)), _const_spec(win_bf.shape), _const_spec(wpool_bf.shape),
            _const_spec((1, PW)), _const_spec((1, PW)), _const_spec((1, PW)),
            _const_spec(sgu_w[0].shape), _const_spec((CHUNK, SGU_HEADS)), smem, smem,
            _const_spec(wout_bf.shape), _const_spec((1, D)), _const_spec(wrt.shape),
            _const_spec((NE, 1)),
        ],
        out_specs=[
            pl.BlockSpec((TS_MIX * SUBLANES, LANES), lambda i: (i, 0)),
            pl.BlockSpec((TOP_K, TS_MIX), lambda i: (0, i)),
            pl.BlockSpec((TOP_K, TS_MIX), lambda i: (0, i)),
            pl.BlockSpec((TOP_K, TS_MIX), lambda i: (0, i)),
            _const_spec((NE, 128)),
            pl.BlockSpec((1, CARRY_ROWS, PW), lambda i: (jnp.minimum(i, last_prompt) // NT, 0, 0)),
            _const_spec((NL, NB, PW)),
            _const_spec((NL, NB, PW)),
        ],
        out_shape=[
            jax.ShapeDtypeStruct((T * SUBLANES, LANES), F32),
            jax.ShapeDtypeStruct((TOP_K, T), I32),
            jax.ShapeDtypeStruct((TOP_K, T), F32),
            jax.ShapeDtypeStruct((TOP_K, T), I32),
            jax.ShapeDtypeStruct((NE, 128), F32),
            jax.ShapeDtypeStruct((B, CARRY_ROWS, PW), F32),
            jax.ShapeDtypeStruct((NL, NB, PW), F32),
            jax.ShapeDtypeStruct((NL, NB, PW), F32),
        ],
        scratch_shapes=[pltpu.VMEM((CARRY_ROWS, PW), F32), pltpu.VMEM((NE, 128), F32),
                        pltpu.VMEM((TS_MIX, PW), BF16)],
        compiler_params=cparams(dimension_semantics=("arbitrary",)),
        name="mixer",
    )(x_prompt.reshape(TP, D), xs_rows, hist, nm, win_bf, wpool_bf, pscale, lng, lnb, sgu_w[0],
      sgu_b[0].T, sw_small, sb_small, wout_bf, nf, wrt, br)

    A = T * TOP_K
    n_blocks = -(-A // TM_MOE) + NE
    n_slots = n_blocks * TM_MOE
    cnt_i = counts[:, 0].astype(I32)
    padded = (cnt_i + TM_MOE - 1) // TM_MOE * TM_MOE
    pad_end = jnp.cumsum(padded)
    pad_start = pad_end - padded
    n_used = (pad_end[-1] // TM_MOE).astype(I32).reshape(1)
    blk = jnp.minimum(jnp.arange(n_blocks, dtype=I32), n_used[0] - 1)
    block_e = jnp.minimum(jnp.sum(pad_end[None, :] <= (blk * TM_MOE)[:, None], axis=1), NE - 1).astype(I32)
    expert_ids = jnp.arange(NE, dtype=I32)[:, None, None]
    dest = jnp.sum(jnp.where(idx_t[None] == expert_ids, pad_start[:, None, None], 0), axis=0) + rank_t

    xs_sorted = pl.pallas_call(
        _scatter_body,
        grid_spec=pltpu.PrefetchScalarGridSpec(
            num_scalar_prefetch=3,
            grid=(T // TS_SCATTER,),
            in_specs=[pl.BlockSpec((TOP_K, TS_SCATTER), lambda i, *_: (0, i), memory_space=pltpu.SMEM),
                      pl.BlockSpec((TS_SCATTER * SUBLANES, LANES), lambda i, *_: (i, 0))],
            out_specs=any_spec,
            scratch_shapes=[pltpu.VMEM((TM_MOE * SUBLANES, LANES), F32), pltpu.SemaphoreType.DMA(()),
                            pltpu.SemaphoreType.DMA(())],
        ),
        out_shape=jax.ShapeDtypeStruct(((n_slots + TM_MOE) * SUBLANES, LANES), F32),
        compiler_params=cparams(dimension_semantics=("arbitrary",)),
        name="moe_scatter",
    )(pad_start, cnt_i, n_used, dest, x1_all)

    ys_sorted = pl.pallas_call(
        _experts_body,
        grid_spec=pltpu.PrefetchScalarGridSpec(
            num_scalar_prefetch=2,
            grid=(n_blocks,),
            in_specs=[
                pl.BlockSpec((TM_MOE * SUBLANES, LANES), lambda b, be, nu: (jnp.minimum(b, nu[0] - 1), 0)),
                pl.BlockSpec((1, D), lambda b, be, nu: (0, 0)),
                pl.BlockSpec((1, D, 2 * DFF), lambda b, be, nu: (be[b], 0, 0)),
                pl.BlockSpec((1, 1, 2 * DFF), lambda b, be, nu: (be[b], 0, 0)),
                pl.BlockSpec((1, DFF, D), lambda b, be, nu: (be[b], 0, 0)),
                pl.BlockSpec((1, 1, D), lambda b, be, nu: (be[b], 0, 0)),
            ],
            out_specs=pl.BlockSpec((TM_MOE * SUBLANES, LANES), lambda b, be, nu: (b, 0)),
            scratch_shapes=[pltpu.VMEM((D, 2 * DFF), BF16), pltpu.VMEM((DFF, D), BF16)],
        ),
        out_shape=jax.ShapeDtypeStruct((n_slots * SUBLANES, LANES), F32),
        compiler_params=cparams(dimension_semantics=("arbitrary",)),
        name="moe_experts",
    )(block_e, n_used, xs_sorted, nf, w_up[0], b_up[0].reshape(NE, 1, 2 * DFF), w_down[0],
      b_down[0].reshape(NE, 1, D))

    p_rows = jnp.concatenate([p_prompt[0].reshape(TP, PLE),
                              p_sample[0].transpose(1, 0, 2).reshape(TSM, PLE)], axis=0)
    n_tiles = T // TS_COMBINE
    npt = TP // TS_COMBINE
    y_p, y_s = pl.pallas_call(
        functools.partial(_combine_body, n_prompt_tiles=npt),
        grid=(n_tiles,),
        in_specs=[
            pl.BlockSpec((TOP_K, TS_COMBINE), lambda i: (0, i), memory_space=pltpu.SMEM),
            pl.BlockSpec((TOP_K, TS_COMBINE), lambda i: (0, jnp.minimum(i + 1, n_tiles - 1)),
                         memory_space=pltpu.SMEM),
            any_spec,
            pl.BlockSpec((TS_COMBINE * SUBLANES, LANES), lambda i: (i, 0)),
            pl.BlockSpec((TS_COMBINE, TOP_K), lambda i: (i, 0)),
            pl.BlockSpec((TS_COMBINE, PLE), lambda i: (i, 0)),
            _const_spec(wg_bf.shape), _const_spec(wp_bf.shape), _const_spec((1, D)),
        ],
        out_specs=[
            pl.BlockSpec((TS_COMBINE, D), lambda i: (jnp.minimum(i, npt - 1), 0)),
            pl.BlockSpec((TS_COMBINE, D), lambda i: (jnp.maximum(i - npt, 0), 0)),
        ],
        out_shape=[jax.ShapeDtypeStruct((TP, D), F32), jax.ShapeDtypeStruct((TSM, D), F32)],
        scratch_shapes=[pltpu.VMEM((2, TOP_K, TS_COMBINE * SUBLANES, LANES), F32),
                        pltpu.SemaphoreType.DMA((2,))],
        compiler_params=cparams(dimension_semantics=("arbitrary",)),
        name="moe_combine",
    )(dest, dest, ys_sorted, x1_all, w_t.T, p_rows, wg_bf, wp_bf, nfin)

    y_prompt = y_p.reshape(B, L, D)
    y_sample = y_s.reshape(NL, NB, D).transpose(1, 0, 2)
    new_pool_prompt = pool_tail[:, CARRY_ROWS - POOL_BUF:, :][None]
    new_pool_sample = jnp.concatenate([state_pool[0][:, NL:, :], zp_s.transpose(1, 0, 2)], axis=1)[None]
    new_sgu_v_sample = vn_s.transpose(1, 0, 2)[None]
    return (y_prompt, y_sample, new_pool_prompt, new_pool_sample, new_sgu_v_sample)
```

```python
import functools

import jax
import jax.numpy as jnp
from jax import lax
from jax.experimental import pallas as pl
from jax.experimental.pallas import tpu as pltpu

F32 = jnp.float32
BF16 = jnp.bfloat16
I32 = jnp.int32

POOL_WINDOWS = (2, 4, 8, 16)
POOL_BUF = max(POOL_WINDOWS) - 1
SGU_HEADS = 4
CHUNK = 128
TOP_K = 4
SWIGLU_LIMIT = 7.0
SWIGLU_ALPHA = 1.702
RMS_EPS = 1e-6
LN_EPS = 1e-5

SUBLANES = 8
LANES = 128
CARRY_ROWS = 16
TS_MIX = 512
TS_SCATTER = 512
DMA_PRIORITIES = 2
TM_MOE = 512
FF_CHUNK = 512
TS_COMBINE = 256
VMEM_LIMIT_BYTES = 56 * 1024 * 1024


def _rms(x, g):
    return x * lax.rsqrt(jnp.mean(x * x, axis=-1, keepdims=True) + RMS_EPS) * g


def _dot(a, b):
    return jnp.dot(a, b, preferred_element_type=F32)


def _load_rows(ref, n_rows):
    return jnp.concatenate([ref[pl.ds(c, n_rows, stride=SUBLANES), :] for c in range(SUBLANES)], axis=-1)


def _store_rows(ref, val):
    n_rows = val.shape[0]
    for c in range(SUBLANES):
        ref[pl.ds(c, n_rows, stride=SUBLANES), :] = val[:, c * LANES:(c + 1) * LANES]


def _gelu(x):
    return 0.5 * x * (1.0 + lax.erf(x * (2.0 ** -0.5)))


def _head_layernorm(v, g, b, hd):
    outs = []
    for h in range(SGU_HEADS):
        vh = v[:, h * hd:(h + 1) * hd]
        mu = jnp.mean(vh, axis=-1, keepdims=True)
        d = vh - mu
        var = jnp.mean(d * d, axis=-1, keepdims=True)
        outs.append(d * lax.rsqrt(var + LN_EPS) * g[:, h * hd:(h + 1) * hd] + b[:, h * hd:(h + 1) * hd])
    return outs


def _pool_project(diffs, wpool_ref, pscale):
    gd = diffs[0].shape[-1]
    outs = []
    for g, d in enumerate(diffs):
        outs.append(_dot(d.astype(BF16), wpool_ref[g]) * pscale[:, g * gd:(g + 1) * gd])
    return jnp.concatenate(outs, axis=-1).astype(BF16)


def _route_and_store(x, a_bf, b_bf, wout_ref, nf_ref, wrt_ref, br_ref, cnt_ref,
                     x1_ref, idx_ref, w_ref, rank_ref, cnt_out_ref):
    ts = x.shape[0]
    pw = a_bf.shape[-1]
    n_exp = wrt_ref.shape[0]
    mix = _dot(a_bf, wout_ref[:pw, :]) + _dot(b_bf, wout_ref[pw:, :])
    x1 = x + mix
    _store_rows(x1_ref, x1)
    h2 = _rms(x1, nf_ref[...])
    logits = lax.dot_general(wrt_ref[...], h2, (((1,), (1,)), ((), ())),
                             precision=lax.Precision.HIGHEST,
                             preferred_element_type=F32) + br_ref[...]
    eio = lax.broadcasted_iota(I32, (n_exp, ts), 0).astype(F32)
    l = logits
    tops, sels = [], []
    for _ in range(TOP_K):
        m = jnp.max(l, axis=0, keepdims=True)
        sel = jnp.min(jnp.where(l == m, eio, float(n_exp)), axis=0, keepdims=True)
        tops.append(m)
        sels.append(sel)
        l = jnp.where(eio == sel, -jnp.inf, l)
    exps = [jnp.exp(t - tops[0]) for t in tops]
    denom = exps[0] + exps[1] + exps[2] + exps[3]
    onehot = jnp.zeros((n_exp, ts), F32)
    for sel in sels:
        onehot = onehot + jnp.where(eio == sel, 1.0, 0.0)
    r_i = lax.broadcasted_iota(I32, (ts, ts), 0)
    c_i = lax.broadcasted_iota(I32, (ts, ts), 1)
    upper = jnp.where(r_i < c_i, 1.0, 0.0).astype(BF16)
    base = cnt_ref[:, 0:1]
    rank_all = _dot(onehot.astype(BF16), upper) + base
    for k in range(TOP_K):
        idx_ref[k:k + 1, :] = sels[k].astype(I32)
        w_ref[k:k + 1, :] = exps[k] / denom
        rk = jnp.sum(jnp.where(eio == sels[k], rank_all, 0.0), axis=0, keepdims=True)
        rank_ref[k:k + 1, :] = rk.astype(I32)
    cnt_ref[...] = cnt_ref[...] + jnp.sum(onehot, axis=1, keepdims=True)
    cnt_out_ref[...] = cnt_ref[...]


def _prompt_tile(t, x_ref, nm_ref, win_ref, wpool_ref, pscale_ref, lng_ref, lnb_ref, sw_ref, sbt_ref,
                 tail_ref, carry_ref, bo_ref, n_tiles, route):
    ts = x_ref.shape[0]
    pw = pscale_ref.shape[-1]
    gd = pw // len(POOL_WINDOWS)
    hd = pw // SGU_HEADS

    @pl.when(t == 0)
    def _():
        carry_ref[...] = jnp.zeros_like(carry_ref)

    x = x_ref[...]
    h = _rms(x, nm_ref[...])
    z = _dot(h.astype(BF16), win_ref[...])
    zp = z[:, :pw]

    ext = jnp.concatenate([carry_ref[...], zp], axis=0)
    carry_ref[...] = zp[ts - CARRY_ROWS:, :]

    @pl.when(t == n_tiles - 1)
    def _():
        tail_ref[0] = zp[ts - CARRY_ROWS:, :]

    pos = t * ts + lax.broadcasted_iota(I32, (ts, 1), 0)
    diffs = []
    for g, w in enumerate(POOL_WINDOWS):
        acc = ext[:, g * gd:(g + 1) * gd]
        k = 1
        while k < w:
            acc = acc + pltpu.roll(acc, k, 0)
            k *= 2
        cnt = jnp.minimum(w, pos + 1).astype(F32)
        diffs.append(acc[CARRY_ROWS:, :] / cnt - zp[:, g * gd:(g + 1) * gd])
    a_bf = _pool_project(diffs, wpool_ref, pscale_ref[...])

    zuv = _gelu(z[:, pw:])
    u = zuv[:, :pw]
    vn = _head_layernorm(zuv[:, pw:], lng_ref[...], lnb_ref[...], hd)
    ri = lax.broadcasted_iota(I32, (CHUNK, CHUNK), 0)
    ci = lax.broadcasted_iota(I32, (CHUNK, CHUNK), 1)
    for hh in range(SGU_HEADS):
        wm = jnp.where(ci <= ri, sw_ref[hh], 0.0).astype(BF16)
        bias = sbt_ref[:, hh:hh + 1]
        vh = vn[hh].astype(BF16)
        for c in range(ts // CHUNK):
            rows = slice(c * CHUNK, (c + 1) * CHUNK)
            mixed = _dot(wm, vh[rows, :]) + bias
            bo_ref[rows, hh * hd:(hh + 1) * hd] = (u[rows, hh * hd:(hh + 1) * hd] * mixed).astype(BF16)
    route(x, a_bf, bo_ref[...])


def _sample_tile(x_ref, st_ref, nm_ref, win_ref, wpool_ref, pscale_ref, lng_ref, lnb_ref, sw_ref,
                 sb_ref, zp_ref, vn_ref, route):
    n_new, nb, pw = zp_ref.shape
    gd = pw // len(POOL_WINDOWS)
    hd = pw // SGU_HEADS

    x = x_ref[...]
    h = _rms(x, nm_ref[...])
    z = _dot(h.astype(BF16), win_ref[...])
    zp = z[:, :pw]
    for l in range(n_new):
        zp_ref[l] = zp[l * nb:(l + 1) * nb, :]

    diffs = []
    for g, w in enumerate(POOL_WINDOWS):
        cols = slice(g * gd, (g + 1) * gd)
        lo_needed = POOL_BUF + 1 - w
        suffix = {}
        run = None
        for r in range(POOL_BUF - 1, lo_needed - 1, -1):
            slab = st_ref[r][:, cols]
            run = slab if run is None else run + slab
            suffix[r] = run
        pieces = []
        for l in range(n_new):
            first_hist = POOL_BUF + 1 + l - w
            tot = suffix[first_hist] if first_hist < POOL_BUF else None
            for l2 in range(max(0, l - w + 1), l + 1):
                zl = zp[l2 * nb:(l2 + 1) * nb, cols]
                tot = zl if tot is None else tot + zl
            pieces.append(tot / float(w) - zp[l * nb:(l + 1) * nb, cols])
        diffs.append(jnp.concatenate(pieces, axis=0))
    a_bf = _pool_project(diffs, wpool_ref, pscale_ref[...])

    zuv = _gelu(z[:, pw:])
    u = zuv[:, :pw]
    vn = _head_layernorm(zuv[:, pw:], lng_ref[...], lnb_ref[...], hd)
    vfull = jnp.concatenate(vn, axis=-1)
    for l in range(n_new):
        vn_ref[l] = vfull[l * nb:(l + 1) * nb, :]
    b_cols = []
    for hh in range(SGU_HEADS):
        pieces = []
        for l in range(n_new):
            mixed = None
            for l2 in range(l + 1):
                term = sw_ref[(hh * n_new + l) * n_new + l2] * vn[hh][l2 * nb:(l2 + 1) * nb, :]
                mixed = term if mixed is None else mixed + term
            mixed = mixed + sb_ref[hh * n_new + l]
            pieces.append(u[l * nb:(l + 1) * nb, hh * hd:(hh + 1) * hd] * mixed)
        b_cols.append(jnp.concatenate(pieces, axis=0))
    route(x, a_bf, jnp.concatenate(b_cols, axis=-1).astype(BF16))


def _mixer_body(xp_ref, xs_ref, st_ref, nm_ref, win_ref, wpool_ref, pscale_ref, lng_ref, lnb_ref,
                sw_ref, sbt_ref, sws_ref, sbs_ref, wout_ref, nf_ref, wrt_ref, br_ref,
                x1_ref, idx_ref, w_ref, rank_ref, cnt_out_ref, tail_ref, zp_ref, vn_ref,
                carry_ref, cnt_ref, bo_ref, *, tiles_per_prompt):
    i = pl.program_id(0)
    n_prompt_steps = pl.num_programs(0) - 1

    @pl.when(i == 0)
    def _():
        cnt_ref[...] = jnp.zeros_like(cnt_ref)

    route = functools.partial(_route_and_store, wout_ref=wout_ref, nf_ref=nf_ref, wrt_ref=wrt_ref,
                              br_ref=br_ref, cnt_ref=cnt_ref, x1_ref=x1_ref, idx_ref=idx_ref,
                              w_ref=w_ref, rank_ref=rank_ref, cnt_out_ref=cnt_out_ref)

    @pl.when(i < n_prompt_steps)
    def _():
        _prompt_tile(lax.rem(i, tiles_per_prompt), xp_ref, nm_ref, win_ref, wpool_ref, pscale_ref,
                     lng_ref, lnb_ref, sw_ref, sbt_ref, tail_ref, carry_ref, bo_ref,
                     tiles_per_prompt, route)

    @pl.when(i == n_prompt_steps)
    def _():
        _sample_tile(xs_ref, st_ref, nm_ref, win_ref, wpool_ref, pscale_ref, lng_ref, lnb_ref,
                     sws_ref, sbs_ref, zp_ref, vn_ref, route)


def _row_copy(src, src_row, dst, dst_row, sem):
    s_rows = pl.ds(pl.multiple_of(src_row * SUBLANES, SUBLANES), SUBLANES)
    d_rows = pl.ds(pl.multiple_of(dst_row * SUBLANES, SUBLANES), SUBLANES)
    return pltpu.make_async_copy(src.at[s_rows], dst.at[d_rows], sem)


def _scatter_body(pstart_ref, pcnt_ref, nused_ref, dest_ref, x1_ref, xs_hbm, zero_ref, zsem, rsem):
    i = pl.program_id(0)
    ts = dest_ref.shape[1]
    tm = zero_ref.shape[0] // SUBLANES
    n_exp = pstart_ref.shape[0]

    def zero_block(row):
        rows = pl.ds(pl.multiple_of(row * SUBLANES, SUBLANES), tm * SUBLANES)
        return pltpu.make_async_copy(zero_ref, xs_hbm.at[rows], zsem)

    @pl.when(i == 0)
    def _():
        zero_ref[...] = jnp.zeros_like(zero_ref)
        for e in range(n_exp):
            zero_block(pstart_ref[e] + pcnt_ref[e]).start()
        for e in range(n_exp):
            zero_block(pstart_ref[e] + pcnt_ref[e]).wait()

        n_total = xs_hbm.shape[0] // (tm * SUBLANES)

        def start_tail(b, c):
            zero_block(b * tm).start()
            return c

        def wait_tail(b, c):
            zero_block(b * tm).wait()
            return c

        lax.fori_loop(nused_ref[0], n_total, start_tail, 0)
        lax.fori_loop(nused_ref[0], n_total, wait_tail, 0)

    def start_rows(r, carry):
        for k in range(TOP_K):
            _row_copy(x1_ref, r, xs_hbm, dest_ref[k, r], rsem).start(priority=k % DMA_PRIORITIES)
        return carry

    lax.fori_loop(0, ts, start_rows, 0, unroll=8)

    def wait_rows(r, carry):
        for k in range(TOP_K):
            _row_copy(x1_ref, 0, xs_hbm, 0, rsem).wait()
        return carry

    lax.fori_loop(0, ts, wait_rows, 0, unroll=8)


def _experts_body(be_ref, nused_ref, xs_ref, nf_ref, wup_ref, bup_ref, wdn_ref, bdn_ref, ys_ref,
                  wup_bf, wdn_bf):
    b = pl.program_id(0)
    d_ff = wdn_ref.shape[1]

    @pl.when(b < nused_ref[0])
    def _():
        prev = be_ref[jnp.maximum(b - 1, 0)]

        @pl.when((b == 0) | (be_ref[b] != prev))
        def _():
            rows = 128

            def cast_up(j, c):
                sl = pl.ds(pl.multiple_of(j * rows, rows), rows)
                wup_bf[sl, :] = wup_ref[0, sl, :].astype(BF16)
                return c

            lax.fori_loop(0, wup_ref.shape[1] // rows, cast_up, 0)

            def cast_dn(j, c):
                sl = pl.ds(pl.multiple_of(j * rows, rows), rows)
                wdn_bf[sl, :] = wdn_ref[0, sl, :].astype(BF16)
                return c

            lax.fori_loop(0, wdn_ref.shape[1] // rows, cast_dn, 0)

        xb = _rms(_load_rows(xs_ref, ys_ref.shape[0] // SUBLANES), nf_ref[...]).astype(BF16)
        acc = None
        for j in range(0, d_ff, FF_CHUNK):
            glu = _dot(xb, wup_bf[:, j:j + FF_CHUNK]) + bup_ref[0, :, j:j + FF_CHUNK]
            lin = _dot(xb, wup_bf[:, d_ff + j:d_ff + j + FF_CHUNK]) + bup_ref[0, :, d_ff + j:d_ff + j + FF_CHUNK]
            glu = jnp.minimum(glu, SWIGLU_LIMIT)
            lin = jnp.clip(lin, -SWIGLU_LIMIT, SWIGLU_LIMIT)
            act = glu * jax.nn.sigmoid(SWIGLU_ALPHA * glu) * (lin + 1.0)
            part = _dot(act.astype(BF16), wdn_bf[j:j + FF_CHUNK, :])
            acc = part if acc is None else acc + part
        _store_rows(ys_ref, acc + bdn_ref[0])

    @pl.when(b >= nused_ref[0])
    def _():
        ys_ref[...] = jnp.zeros_like(ys_ref)


def _combine_body(dcur_ref, dnext_ref, ys_hbm, x1_ref, wt_ref, p_ref, wg_ref, wp_ref, nfin_ref,
                  yp_ref, ysm_ref, buf, sem, *, n_prompt_tiles):
    i = pl.program_id(0)
    n = pl.num_programs(0)
    ts = x1_ref.shape[0] // SUBLANES
    slot = i % 2

    def issue(dref, sl):
        def body(r, c):
            for k in range(TOP_K):
                _row_copy(ys_hbm, dref[k, r], buf.at[sl, k], r, sem.at[sl]).start(priority=k % DMA_PRIORITIES)
            return c
        lax.fori_loop(0, ts, body, 0, unroll=8)

    @pl.when(i == 0)
    def _():
        issue(dcur_ref, 0)

    @pl.when(i + 1 < n)
    def _():
        issue(dnext_ref, 1 - slot)

    def wait_rows(r, c):
        for k in range(TOP_K):
            _row_copy(ys_hbm, 0, buf.at[slot, k], 0, sem.at[slot]).wait()
        return c

    lax.fori_loop(0, ts, wait_rows, 0, unroll=8)

    acc = _load_rows(x1_ref, ts)
    wt = wt_ref[...]
    for k in range(TOP_K):
        acc = acc + _load_rows(buf.at[slot, k], ts) * wt[:, k:k + 1]
    gate = jax.nn.sigmoid(_dot(acc.astype(BF16), wg_ref[...]))
    proj = _dot(p_ref[...].astype(BF16), wp_ref[...])
    y = _rms(acc + gate * proj, nfin_ref[...])

    @pl.when(i < n_prompt_tiles)
    def _():
        yp_ref[...] = y

    @pl.when(i >= n_prompt_tiles)
    def _():
        ysm_ref[...] = y


def _const_spec(shape):
    nd = len(shape)
    return pl.BlockSpec(shape, lambda *_: (0,) * nd)


def kernel(x_prompt, x_sample, state_pool, p_prompt, p_sample, norm_mix, w_in, w_pool, pool_scale,
           sgu_ln_g, sgu_ln_b, sgu_w, sgu_b, w_out, norm_ffn, w_router, b_router, w_up, b_up,
           w_down, b_down, w_ple_gate, w_ple_proj, norm_final):
    B, L, D = x_prompt.shape
    NB, NL, _ = x_sample.shape
    PW = state_pool.shape[-1]
    NE = w_router.shape[-1]
    DFF = w_down.shape[2]
    PLE = p_prompt.shape[-1]
    TP = B * L
    TSM = NB * NL
    T = TP + TSM
    assert w_in.shape[0] == 1, "single trunk layer"
    assert L % TS_MIX == 0 and TSM == TS_MIX and TS_MIX % CHUNK == 0
    assert T % TS_SCATTER == 0 and TP % TS_COMBINE == 0 and TSM % TS_COMBINE == 0
    assert D == SUBLANES * LANES, "a token row must fill exactly one vreg tile"
    assert state_pool.shape[2] == POOL_BUF and PW % len(POOL_WINDOWS) == 0 and PW % SGU_HEADS == 0

    nm = norm_mix[0].reshape(1, D)
    nf = norm_ffn[0].reshape(1, D)
    nfin = norm_final.reshape(1, D)
    win_bf = w_in[0].astype(BF16)
    wpool_bf = w_pool[0].astype(BF16)
    pscale = pool_scale[0].reshape(1, PW)
    lng = sgu_ln_g[0].reshape(1, PW)
    lnb = sgu_ln_b[0].reshape(1, PW)
    wout_bf = w_out[0].astype(BF16)
    wrt = w_router[0].T
    br = b_router[0].reshape(NE, 1)
    wg_bf = w_ple_gate[0].astype(BF16)
    wp_bf = w_ple_proj[0].astype(BF16)

    cparams = functools.partial(pltpu.CompilerParams, vmem_limit_bytes=VMEM_LIMIT_BYTES)
    NT = L // TS_MIX
    n_prompt_steps = B * NT
    xs_rows = x_sample.transpose(1, 0, 2).reshape(TSM, D)
    hist = state_pool[0].transpose(1, 0, 2)
    sw_small = sgu_w[0, :, :NL, :NL].reshape(-1)
    sb_small = sgu_b[0, :, :NL].reshape(-1)
    smem = pl.BlockSpec(memory_space=pltpu.SMEM)
    any_spec = pl.BlockSpec(memory_space=pl.ANY)
    last_prompt = n_prompt_steps - 1
    x1_all, idx_t, w_t, rank_t, counts, pool_tail, zp_s, vn_s = pl.pallas_call(
        functools.partial(_mixer_body, tiles_per_prompt=NT),
        grid=(n_prompt_steps + 1,),
        in_specs=[
            pl.BlockSpec((TS_MIX, D), lambda i: (jnp.minimum(i, last_prompt), 0)),
            _const_spec((TSM, D)), _const_spec(hist.shape),
            _const_spec((1, D)), _const_spec(win_bf.shape), _const_spec(wpool_bf.shape),
            _const_spec((1, PW)), _const_spec((1, PW)), _const_spec((1, PW)),
            _const_spec(sgu_w[0].shape), _const_spec((CHUNK, SGU_HEADS)), smem, smem,
            _const_spec(wout_bf.shape), _const_spec((1, D)), _const_spec(wrt.shape),
            _const_spec((NE, 1)),
        ],
        out_specs=[
            pl.BlockSpec((TS_MIX * SUBLANES, LANES), lambda i: (i, 0)),
            pl.BlockSpec((TOP_K, TS_MIX), lambda i: (0, i)),
            pl.BlockSpec((TOP_K, TS_MIX), lambda i: (0, i)),
            pl.BlockSpec((TOP_K, TS_MIX), lambda i: (0, i)),
            _const_spec((NE, 128)),
            pl.BlockSpec((1, CARRY_ROWS, PW), lambda i: (jnp.minimum(i, last_prompt) // NT, 0, 0)),
            _const_spec((NL, NB, PW)),
            _const_spec((NL, NB, PW)),
        ],
        out_shape=[
            jax.ShapeDtypeStruct((T * SUBLANES, LANES), F32),
            jax.ShapeDtypeStruct((TOP_K, T), I32),
            jax.ShapeDtypeStruct((TOP_K, T), F32),
            jax.ShapeDtypeStruct((TOP_K, T), I32),
            jax.ShapeDtypeStruct((NE, 128), F32),
            jax.ShapeDtypeStruct((B, CARRY_ROWS, PW), F32),
            jax.ShapeDtypeStruct((NL, NB, PW), F32),
            jax.ShapeDtypeStruct((NL, NB, PW), F32),
        ],
        scratch_shapes=[pltpu.VMEM((CARRY_ROWS, PW), F32), pltpu.VMEM((NE, 128), F32),
                        pltpu.VMEM((TS_MIX, PW), BF16)],
        compiler_params=cparams(dimension_semantics=("arbitrary",)),
        name="mixer",
    )(x_prompt.reshape(TP, D), xs_rows, hist, nm, win_bf, wpool_bf, pscale, lng, lnb, sgu_w[0],
      sgu_b[0].T, sw_small, sb_small, wout_bf, nf, wrt, br)

    A = T * TOP_K
    n_blocks = -(-A // TM_MOE) + NE
    n_slots = n_blocks * TM_MOE
    cnt_i = counts[:, 0].astype(I32)
    padded = (cnt_i + TM_MOE - 1) // TM_MOE * TM_MOE
    pad_end = jnp.cumsum(padded)
    pad_start = pad_end - padded
    n_used = (pad_end[-1] // TM_MOE).astype(I32).reshape(1)
    blk = jnp.minimum(jnp.arange(n_blocks, dtype=I32), n_used[0] - 1)
    block_e = jnp.minimum(jnp.sum(pad_end[None, :] <= (blk * TM_MOE)[:, None], axis=1), NE - 1).astype(I32)
    expert_ids = jnp.arange(NE, dtype=I32)[:, None, None]
    dest = jnp.sum(jnp.where(idx_t[None] == expert_ids, pad_start[:, None, None], 0), axis=0) + rank_t

    xs_sorted = pl.pallas_call(
        _scatter_body,
        grid_spec=pltpu.PrefetchScalarGridSpec(
            num_scalar_prefetch=3,
            grid=(T // TS_SCATTER,),
            in_specs=[pl.BlockSpec((TOP_K, TS_SCATTER), lambda i, *_: (0, i), memory_space=pltpu.SMEM),
                      pl.BlockSpec((TS_SCATTER * SUBLANES, LANES), lambda i, *_: (i, 0))],
            out_specs=any_spec,
            scratch_shapes=[pltpu.VMEM((TM_MOE * SUBLANES, LANES), F32), pltpu.SemaphoreType.DMA(()),
                            pltpu.SemaphoreType.DMA(())],
        ),
        out_shape=jax.ShapeDtypeStruct(((n_slots + TM_MOE) * SUBLANES, LANES), F32),
        compiler_params=cparams(dimension_semantics=("arbitrary",)),
        name="moe_scatter",
    )(pad_start, cnt_i, n_used, dest, x1_all)

    ys_sorted = pl.pallas_call(
        _experts_body,
        grid_spec=pltpu.PrefetchScalarGridSpec(
            num_scalar_prefetch=2,
            grid=(n_blocks,),
            in_specs=[
                pl.BlockSpec((TM_MOE * SUBLANES, LANES), lambda b, be, nu: (jnp.minimum(b, nu[0] - 1), 0)),
                pl.BlockSpec((1, D), lambda b, be, nu: (0, 0)),
                pl.BlockSpec((1, D, 2 * DFF), lambda b, be, nu: (be[b], 0, 0)),
                pl.BlockSpec((1, 1, 2 * DFF), lambda b, be, nu: (be[b], 0, 0)),
                pl.BlockSpec((1, DFF, D), lambda b, be, nu: (be[b], 0, 0)),
                pl.BlockSpec((1, 1, D), lambda b, be, nu: (be[b], 0, 0)),
            ],
            out_specs=pl.BlockSpec((TM_MOE * SUBLANES, LANES), lambda b, be, nu: (b, 0)),
            scratch_shapes=[pltpu.VMEM((D, 2 * DFF), BF16), pltpu.VMEM((DFF, D), BF16)],
        ),
        out_shape=jax.ShapeDtypeStruct((n_slots * SUBLANES, LANES), F32),
        compiler_params=cparams(dimension_semantics=("arbitrary",)),
        name="moe_experts",
    )(block_e, n_used, xs_sorted, nf, w_up[0], b_up[0].reshape(NE, 1, 2 * DFF), w_down[0],
      b_down[0].reshape(NE, 1, D))

    p_rows = jnp.concatenate([p_prompt[0].reshape(TP, PLE),
                              p_sample[0].transpose(1, 0, 2).reshape(TSM, PLE)], axis=0)
    n_tiles = T // TS_COMBINE
    npt = TP // TS_COMBINE
    y_p, y_s = pl.pallas_call(
        functools.partial(_combine_body, n_prompt_tiles=npt),
        grid=(n_tiles,),
        in_specs=[
            pl.BlockSpec((TOP_K, TS_COMBINE), lambda i: (0, i), memory_space=pltpu.SMEM),
            pl.BlockSpec((TOP_K, TS_COMBINE), lambda i: (0, jnp.minimum(i + 1, n_tiles - 1)),
                         memory_space=pltpu.SMEM),
            any_spec,
            pl.BlockSpec((TS_COMBINE * SUBLANES, LANES), lambda i: (i, 0)),
            pl.BlockSpec((TS_COMBINE, TOP_K), lambda i: (i, 0)),
            pl.BlockSpec((TS_COMBINE, PLE), lambda i: (i, 0)),
            _const_spec(wg_bf.shape), _const_spec(wp_bf.shape), _const_spec((1, D)),
        ],
        out_specs=[
            pl.BlockSpec((TS_COMBINE, D), lambda i: (jnp.minimum(i, npt - 1), 0)),
            pl.BlockSpec((TS_COMBINE, D), lambda i: (jnp.maximum(i - npt, 0), 0)),
        ],
        out_shape=[jax.ShapeDtypeStruct((TP, D), F32), jax.ShapeDtypeStruct((TSM, D), F32)],
        scratch_shapes=[pltpu.VMEM((2, TOP_K, TS_COMBINE * SUBLANES, LANES), F32),
                        pltpu.SemaphoreType.DMA((2,))],
        compiler_params=cparams(dimension_semantics=("arbitrary",)),
        name="moe_combine",
    )(dest, dest, ys_sorted, x1_all, w_t.T, p_rows, wg_bf, wp_bf, nfin)

    y_prompt = y_p.reshape(B, L, D)
    y_sample = y_s.reshape(NL, NB, D).transpose(1, 0, 2)
    new_pool_prompt = pool_tail[:, CARRY_ROWS - POOL_BUF:, :][None]
    new_pool_sample = jnp.concatenate([state_pool[0][:, NL:, :], zp_s.transpose(1, 0, 2)], axis=1)[None]
    new_sgu_v_sample = vn_s.transpose(1, 0, 2)[None]
    return (y_prompt, y_sample, new_pool_prompt, new_pool_sample, new_sgu_v_sample)
```

```python
import functools

import jax
import jax.numpy as jnp
from jax import lax
from jax.experimental import pallas as pl
from jax.experimental.pallas import tpu as pltpu

F32 = jnp.float32
BF16 = jnp.bfloat16
I32 = jnp.int32

POOL_WINDOWS = (2, 4, 8, 16)
POOL_BUF = max(POOL_WINDOWS) - 1
SGU_HEADS = 4
CHUNK = 128
TOP_K = 4
SWIGLU_LIMIT = 7.0
SWIGLU_ALPHA = 1.702
RMS_EPS = 1e-6
LN_EPS = 1e-5

SUBLANES = 8
LANES = 128
CARRY_ROWS = 16
TS_MIX = 512
TS_SCATTER = 512
DMA_PRIORITIES = 2
TM_MOE = 512
FF_CHUNK = 512
TS_COMBINE = 256
VMEM_LIMIT_BYTES = 56 * 1024 * 1024


def _rms(x, g):
    return x * lax.rsqrt(jnp.mean(x * x, axis=-1, keepdims=True) + RMS_EPS) * g


def _dot(a, b):
    return jnp.dot(a, b, preferred_element_type=F32)


def _load_rows(ref, n_rows):
    return jnp.concatenate([ref[pl.ds(c, n_rows, stride=SUBLANES), :] for c in range(SUBLANES)], axis=-1)


def _store_rows(ref, val):
    n_rows = val.shape[0]
    for c in range(SUBLANES):
        ref[pl.ds(c, n_rows, stride=SUBLANES), :] = val[:, c * LANES:(c + 1) * LANES]


def _gelu(x):
    return 0.5 * x * (1.0 + lax.erf(x * (2.0 ** -0.5)))


def _head_layernorm(v, g, b, hd):
    outs = []
    for h in range(SGU_HEADS):
        vh = v[:, h * hd:(h + 1) * hd]
        mu = jnp.mean(vh, axis=-1, keepdims=True)
        d = vh - mu
        var = jnp.mean(d * d, axis=-1, keepdims=True)
        outs.append(d * lax.rsqrt(var + LN_EPS) * g[:, h * hd:(h + 1) * hd] + b[:, h * hd:(h + 1) * hd])
    return outs


def _pool_project(diffs, wpool_ref, pscale):
    gd = diffs[0].shape[-1]
    outs = []
    for g, d in enumerate(diffs):
        outs.append(_dot(d.astype(BF16), wpool_ref[g]) * pscale[:, g * gd:(g + 1) * gd])
    return jnp.concatenate(outs, axis=-1).astype(BF16)


def _route_and_store(x, a_bf, b_bf, wout_ref, nf_ref, wrt_ref, br_ref, cnt_ref,
                     x1_ref, h2_ref, idx_ref, w_ref, rank_ref, cnt_out_ref):
    ts = x.shape[0]
    pw = a_bf.shape[-1]
    n_exp = wrt_ref.shape[0]
    mix = _dot(a_bf, wout_ref[:pw, :]) + _dot(b_bf, wout_ref[pw:, :])
    x1 = x + mix
    _store_rows(x1_ref, x1)
    h2 = _rms(x1, nf_ref[...])
    _store_rows(h2_ref, h2)
    logits = lax.dot_general(wrt_ref[...], h2, (((1,), (1,)), ((), ())),
                             precision=lax.Precision.HIGHEST,
                             preferred_element_type=F32) + br_ref[...]
    eio = lax.broadcasted_iota(I32, (n_exp, ts), 0).astype(F32)
    l = logits
    tops, sels = [], []
    for _ in range(TOP_K):
        m = jnp.max(l, axis=0, keepdims=True)
        sel = jnp.min(jnp.where(l == m, eio, float(n_exp)), axis=0, keepdims=True)
        tops.append(m)
        sels.append(sel)
        l = jnp.where(eio == sel, -jnp.inf, l)
    exps = [jnp.exp(t - tops[0]) for t in tops]
    denom = exps[0] + exps[1] + exps[2] + exps[3]
    onehot = jnp.zeros((n_exp, ts), F32)
    for sel in sels:
        onehot = onehot + jnp.where(eio == sel, 1.0, 0.0)
    r_i = lax.broadcasted_iota(I32, (ts, ts), 0)
    c_i = lax.broadcasted_iota(I32, (ts, ts), 1)
    upper = jnp.where(r_i < c_i, 1.0, 0.0).astype(BF16)
    base = cnt_ref[:, 0:1]
    rank_all = _dot(onehot.astype(BF16), upper) + base
    for k in range(TOP_K):
        idx_ref[k:k + 1, :] = sels[k].astype(I32)
        w_ref[k:k + 1, :] = exps[k] / denom
        rk = jnp.sum(jnp.where(eio == sels[k], rank_all, 0.0), axis=0, keepdims=True)
        rank_ref[k:k + 1, :] = rk.astype(I32)
    cnt_ref[...] = cnt_ref[...] + jnp.sum(onehot, axis=1, keepdims=True)
    cnt_out_ref[...] = cnt_ref[...]


def _prompt_tile(t, x_ref, nm_ref, win_ref, wpool_ref, pscale_ref, lng_ref, lnb_ref, sw_ref, sbt_ref,
                 tail_ref, carry_ref, bo_ref, n_tiles, route):
    ts = x_ref.shape[0]
    pw = pscale_ref.shape[-1]
    gd = pw // len(POOL_WINDOWS)
    hd = pw // SGU_HEADS

    @pl.when(t == 0)
    def _():
        carry_ref[...] = jnp.zeros_like(carry_ref)

    x = x_ref[...]
    h = _rms(x, nm_ref[...])
    z = _dot(h.astype(BF16), win_ref[...])
    zp = z[:, :pw]

    ext = jnp.concatenate([carry_ref[...], zp], axis=0)
    carry_ref[...] = zp[ts - CARRY_ROWS:, :]

    @pl.when(t == n_tiles - 1)
    def _():
        tail_ref[0] = zp[ts - CARRY_ROWS:, :]

    pos = t * ts + lax.broadcasted_iota(I32, (ts, 1), 0)
    diffs = []
    for g, w in enumerate(POOL_WINDOWS):
        acc = ext[:, g * gd:(g + 1) * gd]
        k = 1
        while k < w:
            acc = acc + pltpu.roll(acc, k, 0)
            k *= 2
        cnt = jnp.minimum(w, pos + 1).astype(F32)
        diffs.append(acc[CARRY_ROWS:, :] / cnt - zp[:, g * gd:(g + 1) * gd])
    a_bf = _pool_project(diffs, wpool_ref, pscale_ref[...])

    zuv = _gelu(z[:, pw:])
    u = zuv[:, :pw]
    vn = _head_layernorm(zuv[:, pw:], lng_ref[...], lnb_ref[...], hd)
    ri = lax.broadcasted_iota(I32, (CHUNK, CHUNK), 0)
    ci = lax.broadcasted_iota(I32, (CHUNK, CHUNK), 1)
    for hh in range(SGU_HEADS):
        wm = jnp.where(ci <= ri, sw_ref[hh], 0.0).astype(BF16)
        bias = sbt_ref[:, hh:hh + 1]
        vh = vn[hh].astype(BF16)
        for c in range(ts // CHUNK):
            rows = slice(c * CHUNK, (c + 1) * CHUNK)
            mixed = _dot(wm, vh[rows, :]) + bias
            bo_ref[rows, hh * hd:(hh + 1) * hd] = (u[rows, hh * hd:(hh + 1) * hd] * mixed).astype(BF16)
    route(x, a_bf, bo_ref[...])


def _sample_tile(x_ref, st_ref, nm_ref, win_ref, wpool_ref, pscale_ref, lng_ref, lnb_ref, sw_ref,
                 sb_ref, zp_ref, vn_ref, route):
    n_new, nb, pw = zp_ref.shape
    gd = pw // len(POOL_WINDOWS)
    hd = pw // SGU_HEADS

    x = x_ref[...]
    h = _rms(x, nm_ref[...])
    z = _dot(h.astype(BF16), win_ref[...])
    zp = z[:, :pw]
    for l in range(n_new):
        zp_ref[l] = zp[l * nb:(l + 1) * nb, :]

    diffs = []
    for g, w in enumerate(POOL_WINDOWS):
        cols = slice(g * gd, (g + 1) * gd)
        lo_needed = POOL_BUF + 1 - w
        suffix = {}
        run = None
        for r in range(POOL_BUF - 1, lo_needed - 1, -1):
            slab = st_ref[r][:, cols]
            run = slab if run is None else run + slab
            suffix[r] = run
        pieces = []
        for l in range(n_new):
            first_hist = POOL_BUF + 1 + l - w
            tot = suffix[first_hist] if first_hist < POOL_BUF else None
            for l2 in range(max(0, l - w + 1), l + 1):
                zl = zp[l2 * nb:(l2 + 1) * nb, cols]
                tot = zl if tot is None else tot + zl
            pieces.append(tot / float(w) - zp[l * nb:(l + 1) * nb, cols])
        diffs.append(jnp.concatenate(pieces, axis=0))
    a_bf = _pool_project(diffs, wpool_ref, pscale_ref[...])

    zuv = _gelu(z[:, pw:])
    u = zuv[:, :pw]
    vn = _head_layernorm(zuv[:, pw:], lng_ref[...], lnb_ref[...], hd)
    vfull = jnp.concatenate(vn, axis=-1)
    for l in range(n_new):
        vn_ref[l] = vfull[l * nb:(l + 1) * nb, :]
    b_cols = []
    for hh in range(SGU_HEADS):
        pieces = []
        for l in range(n_new):
            mixed = None
            for l2 in range(l + 1):
                term = sw_ref[(hh * n_new + l) * n_new + l2] * vn[hh][l2 * nb:(l2 + 1) * nb, :]
                mixed = term if mixed is None else mixed + term
            mixed = mixed + sb_ref[hh * n_new + l]
            pieces.append(u[l * nb:(l + 1) * nb, hh * hd:(hh + 1) * hd] * mixed)
        b_cols.append(jnp.concatenate(pieces, axis=0))
    route(x, a_bf, jnp.concatenate(b_cols, axis=-1).astype(BF16))


def _mixer_body(xp_ref, xs_ref, st_ref, nm_ref, win_ref, wpool_ref, pscale_ref, lng_ref, lnb_ref,
                sw_ref, sbt_ref, sws_ref, sbs_ref, wout_ref, nf_ref, wrt_ref, br_ref,
                x1_ref, h2_ref, idx_ref, w_ref, rank_ref, cnt_out_ref, tail_ref, zp_ref, vn_ref,
                carry_ref, cnt_ref, bo_ref, *, tiles_per_prompt):
    i = pl.program_id(0)
    n_prompt_steps = pl.num_programs(0) - 1

    @pl.when(i == 0)
    def _():
        cnt_ref[...] = jnp.zeros_like(cnt_ref)

    route = functools.partial(_route_and_store, wout_ref=wout_ref, nf_ref=nf_ref, wrt_ref=wrt_ref,
                              br_ref=br_ref, cnt_ref=cnt_ref, x1_ref=x1_ref, h2_ref=h2_ref,
                              idx_ref=idx_ref,
                              w_ref=w_ref, rank_ref=rank_ref, cnt_out_ref=cnt_out_ref)

    @pl.when(i < n_prompt_steps)
    def _():
        _prompt_tile(lax.rem(i, tiles_per_prompt), xp_ref, nm_ref, win_ref, wpool_ref, pscale_ref,
                     lng_ref, lnb_ref, sw_ref, sbt_ref, tail_ref, carry_ref, bo_ref,
                     tiles_per_prompt, route)

    @pl.when(i == n_prompt_steps)
    def _():
        _sample_tile(xs_ref, st_ref, nm_ref, win_ref, wpool_ref, pscale_ref, lng_ref, lnb_ref,
                     sws_ref, sbs_ref, zp_ref, vn_ref, route)


def _row_copy(src, src_row, dst, dst_row, sem):
    s_rows = pl.ds(pl.multiple_of(src_row * SUBLANES, SUBLANES), SUBLANES)
    d_rows = pl.ds(pl.multiple_of(dst_row * SUBLANES, SUBLANES), SUBLANES)
    return pltpu.make_async_copy(src.at[s_rows], dst.at[d_rows], sem)


def _scatter_body(pstart_ref, pcnt_ref, nused_ref, dest_ref, x1_ref, xs_hbm, zero_ref, zsem, rsem):
    i = pl.program_id(0)
    ts = x1_ref.shape[0] // SUBLANES
    tm = zero_ref.shape[0] // SUBLANES
    n_exp = pstart_ref.shape[0]

    def zero_block(row):
        rows = pl.ds(pl.multiple_of(row * SUBLANES, SUBLANES), tm * SUBLANES)
        return pltpu.make_async_copy(zero_ref, xs_hbm.at[rows], zsem)

    @pl.when(i == 0)
    def _():
        zero_ref[...] = jnp.zeros_like(zero_ref)
        for e in range(n_exp):
            zero_block(pstart_ref[e] + pcnt_ref[e]).start()
        for e in range(n_exp):
            zero_block(pstart_ref[e] + pcnt_ref[e]).wait()

        n_total = xs_hbm.shape[0] // (tm * SUBLANES)

        def start_tail(b, c):
            zero_block(b * tm).start()
            return c

        def wait_tail(b, c):
            zero_block(b * tm).wait()
            return c

        lax.fori_loop(nused_ref[0], n_total, start_tail, 0)
        lax.fori_loop(nused_ref[0], n_total, wait_tail, 0)

    def start_rows(r, carry):
        for k in range(TOP_K):
            _row_copy(x1_ref, r, xs_hbm, dest_ref[0, 0, k * ts + r], rsem).start(priority=k % DMA_PRIORITIES)
        return carry

    lax.fori_loop(0, ts, start_rows, 0, unroll=8)

    def wait_rows(r, carry):
        for k in range(TOP_K):
            _row_copy(x1_ref, 0, xs_hbm, 0, rsem).wait()
        return carry

    lax.fori_loop(0, ts, wait_rows, 0, unroll=8)


def _experts_body(be_ref, nused_ref, eord_ref, eorder_ref, ndist_ref,
                  xs_ref, wup_hbm, bup_ref, wdn_hbm, bdn_ref, ys_ref,
                  wup_f32, wdn_f32, wup_bf, wdn_bf, wsem):
    b = pl.program_id(0)
    d_ff = wdn_bf.shape[0]

    def weight_copies(e, slot):
        return (pltpu.make_async_copy(wup_hbm.at[e], wup_f32.at[slot], wsem.at[slot, 0]),
                pltpu.make_async_copy(wdn_hbm.at[e], wdn_f32.at[slot], wsem.at[slot, 1]))

    @pl.when(b == 0)
    def _():
        for cp in weight_copies(eorder_ref[0], 0):
            cp.start()

    @pl.when(b < nused_ref[0])
    def _():
        prev = be_ref[jnp.maximum(b - 1, 0)]

        @pl.when((b == 0) | (be_ref[b] != prev))
        def _():
            j = eord_ref[b]
            slot = lax.rem(j, 2)
            for cp in weight_copies(be_ref[b], slot):
                cp.wait()

            @pl.when(j + 1 < ndist_ref[0])
            def _():
                for cp in weight_copies(eorder_ref[j + 1], 1 - slot):
                    cp.start()

            rows = 128

            def cast_up(r, c):
                sl = pl.ds(pl.multiple_of(r * rows, rows), rows)
                wup_bf[sl, :] = wup_f32[slot, sl, :].astype(BF16)
                return c

            lax.fori_loop(0, wup_bf.shape[0] // rows, cast_up, 0)

            def cast_dn(r, c):
                sl = pl.ds(pl.multiple_of(r * rows, rows), rows)
                wdn_bf[sl, :] = wdn_f32[slot, sl, :].astype(BF16)
                return c

            lax.fori_loop(0, wdn_bf.shape[0] // rows, cast_dn, 0)

        xb = _load_rows(xs_ref, ys_ref.shape[0] // SUBLANES).astype(BF16)
        acc = None
        for j in range(0, d_ff, FF_CHUNK):
            glu = _dot(xb, wup_bf[:, j:j + FF_CHUNK]) + bup_ref[0, :, j:j + FF_CHUNK]
            lin = _dot(xb, wup_bf[:, d_ff + j:d_ff + j + FF_CHUNK]) + bup_ref[0, :, d_ff + j:d_ff + j + FF_CHUNK]
            glu = jnp.minimum(glu, SWIGLU_LIMIT)
            lin = jnp.clip(lin, -SWIGLU_LIMIT, SWIGLU_LIMIT)
            act = glu * jax.nn.sigmoid(SWIGLU_ALPHA * glu) * (lin + 1.0)
            part = _dot(act.astype(BF16), wdn_bf[j:j + FF_CHUNK, :])
            acc = part if acc is None else acc + part
        _store_rows(ys_ref, acc + bdn_ref[0])

    @pl.when(b >= nused_ref[0])
    def _():
        ys_ref[...] = jnp.zeros_like(ys_ref)


def _combine_body(dcur_ref, dnext_ref, wsm_ref, ys_hbm, x1_ref, p_ref, wg_ref, wp_ref, nfin_ref,
                  yp_ref, ysm_ref, buf, acc_ref, sem, *, n_prompt_tiles):
    i = pl.program_id(0)
    n = pl.num_programs(0)
    ts = x1_ref.shape[0] // SUBLANES
    slot = i % 2

    def issue(dref, sl):
        def body(r, c):
            for k in range(TOP_K):
                cp = _row_copy(ys_hbm, dref[0, 0, k * ts + r], buf.at[sl, k], r, sem.at[sl])
                cp.start(priority=k % DMA_PRIORITIES)
            return c
        lax.fori_loop(0, ts, body, 0, unroll=8)

    @pl.when(i == 0)
    def _():
        issue(dcur_ref, 0)

    @pl.when(i + 1 < n)
    def _():
        issue(dnext_ref, 1 - slot)

    def wait_rows(r, c):
        for k in range(TOP_K):
            _row_copy(ys_hbm, 0, buf.at[slot, k], 0, sem.at[slot]).wait()
        return c

    lax.fori_loop(0, ts, wait_rows, 0, unroll=8)

    def sum_row(r, c):
        rows = pl.ds(pl.multiple_of(r * SUBLANES, SUBLANES), SUBLANES)
        acc = x1_ref[rows, :]
        for k in range(TOP_K):
            acc = acc + buf[slot, k, rows, :] * wsm_ref[0, 0, k * ts + r]
        acc_ref[rows, :] = acc
        return c

    lax.fori_loop(0, ts, sum_row, 0, unroll=8)

    acc = _load_rows(acc_ref, ts)
    gate = jax.nn.sigmoid(_dot(acc.astype(BF16), wg_ref[...]))
    proj = _dot(p_ref[...].astype(BF16), wp_ref[...])
    y = _rms(acc + gate * proj, nfin_ref[...])

    @pl.when(i < n_prompt_tiles)
    def _():
        yp_ref[...] = y

    @pl.when(i >= n_prompt_tiles)
    def _():
        ysm_ref[...] = y


def _const_spec(shape):
    nd = len(shape)
    return pl.BlockSpec(shape, lambda *_: (0,) * nd)


def kernel(x_prompt, x_sample, state_pool, p_prompt, p_sample, norm_mix, w_in, w_pool, pool_scale,
           sgu_ln_g, sgu_ln_b, sgu_w, sgu_b, w_out, norm_ffn, w_router, b_router, w_up, b_up,
           w_down, b_down, w_ple_gate, w_ple_proj, norm_final):
    B, L, D = x_prompt.shape
    NB, NL, _ = x_sample.shape
    PW = state_pool.shape[-1]
    NE = w_router.shape[-1]
    DFF = w_down.shape[2]
    PLE = p_prompt.shape[-1]
    TP = B * L
    TSM = NB * NL
    T = TP + TSM
    assert w_in.shape[0] == 1, "single trunk layer"
    assert L % TS_MIX == 0 and TSM == TS_MIX and TS_MIX % CHUNK == 0
    assert T % TS_SCATTER == 0 and TP % TS_COMBINE == 0 and TSM % TS_COMBINE == 0
    assert D == SUBLANES * LANES, "a token row must fill exactly one vreg tile"
    assert state_pool.shape[2] == POOL_BUF and PW % len(POOL_WINDOWS) == 0 and PW % SGU_HEADS == 0

    nm = norm_mix[0].reshape(1, D)
    nf = norm_ffn[0].reshape(1, D)
    nfin = norm_final.reshape(1, D)
    win_bf = w_in[0].astype(BF16)
    wpool_bf = w_pool[0].astype(BF16)
    pscale = pool_scale[0].reshape(1, PW)
    lng = sgu_ln_g[0].reshape(1, PW)
    lnb = sgu_ln_b[0].reshape(1, PW)
    wout_bf = w_out[0].astype(BF16)
    wrt = w_router[0].T
    br = b_router[0].reshape(NE, 1)
    wg_bf = w_ple_gate[0].astype(BF16)
    wp_bf = w_ple_proj[0].astype(BF16)

    cparams = functools.partial(pltpu.CompilerParams, vmem_limit_bytes=VMEM_LIMIT_BYTES)
    NT = L // TS_MIX
    n_prompt_steps = B * NT
    xs_rows = x_sample.transpose(1, 0, 2).reshape(TSM, D)
    hist = state_pool[0].transpose(1, 0, 2)
    sw_small = sgu_w[0, :, :NL, :NL].reshape(-1)
    sb_small = sgu_b[0, :, :NL].reshape(-1)
    smem = pl.BlockSpec(memory_space=pltpu.SMEM)
    any_spec = pl.BlockSpec(memory_space=pl.ANY)
    last_prompt = n_prompt_steps - 1
    x1_all, h2_all, idx_t, w_t, rank_t, counts, pool_tail, zp_s, vn_s = pl.pallas_call(
        functools.partial(_mixer_body, tiles_per_prompt=NT),
        grid=(n_prompt_steps + 1,),
        in_specs=[
            pl.BlockSpec((TS_MIX, D), lambda i: (jnp.minimum(i, last_prompt), 0)),
            _const_spec((TSM, D)), _const_spec(hist.shape),
            _const_spec((1, D)), _const_spec(win_bf.shape), _const_spec(wpool_bf.shape),
            _const_spec((1, PW)), _const_spec((1, PW)), _const_spec((1, PW)),
            _const_spec(sgu_w[0].shape), _const_spec((CHUNK, SGU_HEADS)), smem, smem,
            _const_spec(wout_bf.shape), _const_spec((1, D)), _const_spec(wrt.shape),
            _const_spec((NE, 1)),
        ],
        out_specs=[
            pl.BlockSpec((TS_MIX * SUBLANES, LANES), lambda i: (i, 0)),
            pl.BlockSpec((TS_MIX * SUBLANES, LANES), lambda i: (i, 0)),
            pl.BlockSpec((TOP_K, TS_MIX), lambda i: (0, i)),
            pl.BlockSpec((TOP_K, TS_MIX), lambda i: (0, i)),
            pl.BlockSpec((TOP_K, TS_MIX), lambda i: (0, i)),
            _const_spec((NE, 128)),
            pl.BlockSpec((1, CARRY_ROWS, PW), lambda i: (jnp.minimum(i, last_prompt) // NT, 0, 0)),
            _const_spec((NL, NB, PW)),
            _const_spec((NL, NB, PW)),
        ],
        out_shape=[
            jax.ShapeDtypeStruct((T * SUBLANES, LANES), F32),
            jax.ShapeDtypeStruct((T * SUBLANES, LANES), F32),
            jax.ShapeDtypeStruct((TOP_K, T), I32),
            jax.ShapeDtypeStruct((TOP_K, T), F32),
            jax.ShapeDtypeStruct((TOP_K, T), I32),
            jax.ShapeDtypeStruct((NE, 128), F32),
            jax.ShapeDtypeStruct((B, CARRY_ROWS, PW), F32),
            jax.ShapeDtypeStruct((NL, NB, PW), F32),
            jax.ShapeDtypeStruct((NL, NB, PW), F32),
        ],
        scratch_shapes=[pltpu.VMEM((CARRY_ROWS, PW), F32), pltpu.VMEM((NE, 128), F32),
                        pltpu.VMEM((TS_MIX, PW), BF16)],
        compiler_params=cparams(dimension_semantics=("arbitrary",)),
        name="mixer",
    )(x_prompt.reshape(TP, D), xs_rows, hist, nm, win_bf, wpool_bf, pscale, lng, lnb, sgu_w[0],
      sgu_b[0].T, sw_small, sb_small, wout_bf, nf, wrt, br)

    A = T * TOP_K
    n_blocks = -(-A // TM_MOE) + NE
    n_slots = n_blocks * TM_MOE
    cnt_i = counts[:, 0].astype(I32)
    padded = (cnt_i + TM_MOE - 1) // TM_MOE * TM_MOE
    pad_end = jnp.cumsum(padded)
    pad_start = pad_end - padded
    n_used = (pad_end[-1] // TM_MOE).astype(I32).reshape(1)
    blk = jnp.minimum(jnp.arange(n_blocks, dtype=I32), n_used[0] - 1)
    block_e = jnp.minimum(jnp.sum(pad_end[None, :] <= (blk * TM_MOE)[:, None], axis=1), NE - 1).astype(I32)
    expert_ids = jnp.arange(NE, dtype=I32)[:, None, None]
    dest = jnp.sum(jnp.where(idx_t[None] == expert_ids, pad_start[:, None, None], 0), axis=0) + rank_t
    used = padded > 0
    ordinal = jnp.cumsum(used.astype(I32)) - 1
    n_distinct = jnp.sum(used.astype(I32)).reshape(1)
    block_ord = jnp.sum(jnp.where(block_e[:, None] == expert_ids[:, 0, 0][None, :], ordinal[None, :], 0), axis=1)
    slots_j = jnp.minimum(jnp.arange(NE + 1, dtype=I32), n_distinct[0] - 1)
    expert_order = jnp.sum(jnp.where(used[None, :] & (ordinal[None, :] == slots_j[:, None]),
                                     expert_ids[:, 0, 0][None, :], 0), axis=1).astype(I32)

    def per_tile(a, ts):
        return a.reshape(TOP_K, T // ts, ts).transpose(1, 0, 2).reshape(T // ts, 1, TOP_K * ts)

    xs_sorted = pl.pallas_call(
        _scatter_body,
        grid_spec=pltpu.PrefetchScalarGridSpec(
            num_scalar_prefetch=3,
            grid=(T // TS_SCATTER,),
            in_specs=[pl.BlockSpec((1, 1, TOP_K * TS_SCATTER), lambda i, *_: (i, 0, 0),
                                   memory_space=pltpu.SMEM),
                      pl.BlockSpec((TS_SCATTER * SUBLANES, LANES), lambda i, *_: (i, 0))],
            out_specs=any_spec,
            scratch_shapes=[pltpu.VMEM((TM_MOE * SUBLANES, LANES), F32), pltpu.SemaphoreType.DMA(()),
                            pltpu.SemaphoreType.DMA(())],
        ),
        out_shape=jax.ShapeDtypeStruct(((n_slots + TM_MOE) * SUBLANES, LANES), F32),
        compiler_params=cparams(dimension_semantics=("arbitrary",)),
        name="moe_scatter",
    )(pad_start, cnt_i, n_used, per_tile(dest, TS_SCATTER), h2_all)

    ys_sorted = pl.pallas_call(
        _experts_body,
        grid_spec=pltpu.PrefetchScalarGridSpec(
            num_scalar_prefetch=5,
            grid=(n_blocks,),
            in_specs=[
                pl.BlockSpec((TM_MOE * SUBLANES, LANES), lambda b, be, nu, *_: (jnp.minimum(b, nu[0] - 1), 0)),
                any_spec,
                pl.BlockSpec((1, 1, 2 * DFF), lambda b, be, *_: (be[b], 0, 0)),
                any_spec,
                pl.BlockSpec((1, 1, D), lambda b, be, *_: (be[b], 0, 0)),
            ],
            out_specs=pl.BlockSpec((TM_MOE * SUBLANES, LANES), lambda b, *_: (b, 0)),
            scratch_shapes=[pltpu.VMEM((2, D, 2 * DFF), F32), pltpu.VMEM((2, DFF, D), F32),
                            pltpu.VMEM((D, 2 * DFF), BF16), pltpu.VMEM((DFF, D), BF16),
                            pltpu.SemaphoreType.DMA((2, 2))],
        ),
        out_shape=jax.ShapeDtypeStruct((n_slots * SUBLANES, LANES), F32),
        compiler_params=cparams(dimension_semantics=("arbitrary",)),
        name="moe_experts",
    )(block_e, n_used, block_ord.astype(I32), expert_order, n_distinct, xs_sorted, w_up[0],
      b_up[0].reshape(NE, 1, 2 * DFF), w_down[0], b_down[0].reshape(NE, 1, D))

    p_rows = jnp.concatenate([p_prompt[0].reshape(TP, PLE),
                              p_sample[0].transpose(1, 0, 2).reshape(TSM, PLE)], axis=0)
    n_tiles = T // TS_COMBINE
    npt = TP // TS_COMBINE
    y_p, y_s = pl.pallas_call(
        functools.partial(_combine_body, n_prompt_tiles=npt),
        grid=(n_tiles,),
        in_specs=[
            pl.BlockSpec((1, 1, TOP_K * TS_COMBINE), lambda i: (i, 0, 0), memory_space=pltpu.SMEM),
            pl.BlockSpec((1, 1, TOP_K * TS_COMBINE), lambda i: (jnp.minimum(i + 1, n_tiles - 1), 0, 0),
                         memory_space=pltpu.SMEM),
            pl.BlockSpec((1, 1, TOP_K * TS_COMBINE), lambda i: (i, 0, 0), memory_space=pltpu.SMEM),
            any_spec,
            pl.BlockSpec((TS_COMBINE * SUBLANES, LANES), lambda i: (i, 0)),
            pl.BlockSpec((TS_COMBINE, PLE), lambda i: (i, 0)),
            _const_spec(wg_bf.shape), _const_spec(wp_bf.shape), _const_spec((1, D)),
        ],
        out_specs=[
            pl.BlockSpec((TS_COMBINE, D), lambda i: (jnp.minimum(i, npt - 1), 0)),
            pl.BlockSpec((TS_COMBINE, D), lambda i: (jnp.maximum(i - npt, 0), 0)),
        ],
        out_shape=[jax.ShapeDtypeStruct((TP, D), F32), jax.ShapeDtypeStruct((TSM, D), F32)],
        scratch_shapes=[pltpu.VMEM((2, TOP_K, TS_COMBINE * SUBLANES, LANES), F32),
                        pltpu.VMEM((TS_COMBINE * SUBLANES, LANES), F32),
                        pltpu.SemaphoreType.DMA((2,))],
        compiler_params=cparams(dimension_semantics=("arbitrary",)),
        name="moe_combine",
    )(per_tile(dest, TS_COMBINE), per_tile(dest, TS_COMBINE), per_tile(w_t, TS_COMBINE), ys_sorted,
      x1_all, p_rows, wg_bf, wp_bf, nfin)

    y_prompt = y_p.reshape(B, L, D)
    y_sample = y_s.reshape(NL, NB, D).transpose(1, 0, 2)
    new_pool_prompt = pool_tail[:, CARRY_ROWS - POOL_BUF:, :][None]
    new_pool_sample = jnp.concatenate([state_pool[0][:, NL:, :], zp_s.transpose(1, 0, 2)], axis=1)[None]
    new_sgu_v_sample = vn_s.transpose(1, 0, 2)[None]
    return (y_prompt, y_sample, new_pool_prompt, new_pool_sample, new_sgu_v_sample)
```

```python
import functools

import jax
import jax.numpy as jnp
from jax import lax
from jax.experimental import pallas as pl
from jax.experimental.pallas import tpu as pltpu

F32 = jnp.float32
BF16 = jnp.bfloat16
I32 = jnp.int32

POOL_WINDOWS = (2, 4, 8, 16)
POOL_BUF = max(POOL_WINDOWS) - 1
SGU_HEADS = 4
CHUNK = 128
TOP_K = 4
SWIGLU_LIMIT = 7.0
SWIGLU_ALPHA = 1.702
RMS_EPS = 1e-6
LN_EPS = 1e-5

SUBLANES = 8
LANES = 128
CARRY_ROWS = 16
TS_MIX = 512
TS_SCATTER = 512
DMA_PRIORITIES = 2
TM_MOE = 512
FF_CHUNK = 512
TS_COMBINE = 512
VMEM_LIMIT_BYTES = 56 * 1024 * 1024


def _rms(x, g):
    return x * lax.rsqrt(jnp.mean(x * x, axis=-1, keepdims=True) + RMS_EPS) * g


def _dot(a, b):
    return jnp.dot(a, b, preferred_element_type=F32)


def _load_rows(ref, n_rows):
    return jnp.concatenate([ref[pl.ds(c, n_rows, stride=SUBLANES), :] for c in range(SUBLANES)], axis=-1)


def _store_rows(ref, val):
    n_rows = val.shape[0]
    for c in range(SUBLANES):
        ref[pl.ds(c, n_rows, stride=SUBLANES), :] = val[:, c * LANES:(c + 1) * LANES]


def _gelu(x):
    return 0.5 * x * (1.0 + lax.erf(x * (2.0 ** -0.5)))


def _head_layernorm(v, g, b, hd):
    outs = []
    for h in range(SGU_HEADS):
        vh = v[:, h * hd:(h + 1) * hd]
        mu = jnp.mean(vh, axis=-1, keepdims=True)
        d = vh - mu
        var = jnp.mean(d * d, axis=-1, keepdims=True)
        outs.append(d * lax.rsqrt(var + LN_EPS) * g[:, h * hd:(h + 1) * hd] + b[:, h * hd:(h + 1) * hd])
    return outs


def _pool_project(diffs, wpool_ref, pscale):
    gd = diffs[0].shape[-1]
    outs = []
    for g, d in enumerate(diffs):
        outs.append(_dot(d.astype(BF16), wpool_ref[g]) * pscale[:, g * gd:(g + 1) * gd])
    return jnp.concatenate(outs, axis=-1).astype(BF16)


def _route_and_store(x, a_bf, b_bf, wout_ref, nf_ref, wrt_ref, br_ref, cnt_ref,
                     x1_ref, h2_ref, idx_ref, w_ref, rank_ref, cnt_out_ref):
    ts = x.shape[0]
    pw = a_bf.shape[-1]
    n_exp = wrt_ref.shape[0]
    mix = _dot(a_bf, wout_ref[:pw, :]) + _dot(b_bf, wout_ref[pw:, :])
    x1 = x + mix
    _store_rows(x1_ref, x1)
    h2 = _rms(x1, nf_ref[...])
    _store_rows(h2_ref, h2)
    logits = lax.dot_general(wrt_ref[...], h2, (((1,), (1,)), ((), ())),
                             precision=lax.Precision.HIGHEST,
                             preferred_element_type=F32) + br_ref[...]
    eio = lax.broadcasted_iota(I32, (n_exp, ts), 0).astype(F32)
    l = logits
    tops, sels = [], []
    for _ in range(TOP_K):
        m = jnp.max(l, axis=0, keepdims=True)
        sel = jnp.min(jnp.where(l == m, eio, float(n_exp)), axis=0, keepdims=True)
        tops.append(m)
        sels.append(sel)
        l = jnp.where(eio == sel, -jnp.inf, l)
    exps = [jnp.exp(t - tops[0]) for t in tops]
    denom = exps[0] + exps[1] + exps[2] + exps[3]
    onehot = jnp.zeros((n_exp, ts), F32)
    for sel in sels:
        onehot = onehot + jnp.where(eio == sel, 1.0, 0.0)
    r_i = lax.broadcasted_iota(I32, (ts, ts), 0)
    c_i = lax.broadcasted_iota(I32, (ts, ts), 1)
    upper = jnp.where(r_i < c_i, 1.0, 0.0).astype(BF16)
    base = cnt_ref[:, 0:1]
    rank_all = _dot(onehot.astype(BF16), upper) + base
    for k in range(TOP_K):
        idx_ref[k:k + 1, :] = sels[k].astype(I32)
        w_ref[k:k + 1, :] = exps[k] / denom
        rk = jnp.sum(jnp.where(eio == sels[k], rank_all, 0.0), axis=0, keepdims=True)
        rank_ref[k:k + 1, :] = rk.astype(I32)
    cnt_ref[...] = cnt_ref[...] + jnp.sum(onehot, axis=1, keepdims=True)
    cnt_out_ref[...] = cnt_ref[...]


def _prompt_tile(t, x_ref, nm_ref, win_ref, wpool_ref, pscale_ref, lng_ref, lnb_ref, sw_ref, sbt_ref,
                 tail_ref, carry_ref, bo_ref, n_tiles, route):
    ts = x_ref.shape[0]
    pw = pscale_ref.shape[-1]
    gd = pw // len(POOL_WINDOWS)
    hd = pw // SGU_HEADS

    @pl.when(t == 0)
    def _():
        carry_ref[...] = jnp.zeros_like(carry_ref)

    x = x_ref[...]
    h = _rms(x, nm_ref[...])
    z = _dot(h.astype(BF16), win_ref[...])
    zp = z[:, :pw]

    ext = jnp.concatenate([carry_ref[...], zp], axis=0)
    carry_ref[...] = zp[ts - CARRY_ROWS:, :]

    @pl.when(t == n_tiles - 1)
    def _():
        tail_ref[0] = zp[ts - CARRY_ROWS:, :]

    pos = t * ts + lax.broadcasted_iota(I32, (ts, 1), 0)
    diffs = []
    for g, w in enumerate(POOL_WINDOWS):
        acc = ext[:, g * gd:(g + 1) * gd]
        k = 1
        while k < w:
            acc = acc + pltpu.roll(acc, k, 0)
            k *= 2
        cnt = jnp.minimum(w, pos + 1).astype(F32)
        diffs.append(acc[CARRY_ROWS:, :] / cnt - zp[:, g * gd:(g + 1) * gd])
    a_bf = _pool_project(diffs, wpool_ref, pscale_ref[...])

    zuv = _gelu(z[:, pw:])
    u = zuv[:, :pw]
    vn = _head_layernorm(zuv[:, pw:], lng_ref[...], lnb_ref[...], hd)
    ri = lax.broadcasted_iota(I32, (CHUNK, CHUNK), 0)
    ci = lax.broadcasted_iota(I32, (CHUNK, CHUNK), 1)
    for hh in range(SGU_HEADS):
        wm = jnp.where(ci <= ri, sw_ref[hh], 0.0).astype(BF16)
        bias = sbt_ref[:, hh:hh + 1]
        vh = vn[hh].astype(BF16)
        for c in range(ts // CHUNK):
            rows = slice(c * CHUNK, (c + 1) * CHUNK)
            mixed = _dot(wm, vh[rows, :]) + bias
            bo_ref[rows, hh * hd:(hh + 1) * hd] = (u[rows, hh * hd:(hh + 1) * hd] * mixed).astype(BF16)
    route(x, a_bf, bo_ref[...])


def _sample_tile(x_ref, st_ref, nm_ref, win_ref, wpool_ref, pscale_ref, lng_ref, lnb_ref, sw_ref,
                 sb_ref, zp_ref, vn_ref, route):
    n_new, nb, pw = zp_ref.shape
    gd = pw // len(POOL_WINDOWS)
    hd = pw // SGU_HEADS

    x = x_ref[...]
    h = _rms(x, nm_ref[...])
    z = _dot(h.astype(BF16), win_ref[...])
    zp = z[:, :pw]
    for l in range(n_new):
        zp_ref[l] = zp[l * nb:(l + 1) * nb, :]

    diffs = []
    for g, w in enumerate(POOL_WINDOWS):
        cols = slice(g * gd, (g + 1) * gd)
        lo_needed = POOL_BUF + 1 - w
        suffix = {}
        run = None
        for r in range(POOL_BUF - 1, lo_needed - 1, -1):
            slab = st_ref[r][:, cols]
            run = slab if run is None else run + slab
            suffix[r] = run
        pieces = []
        for l in range(n_new):
            first_hist = POOL_BUF + 1 + l - w
            tot = suffix[first_hist] if first_hist < POOL_BUF else None
            for l2 in range(max(0, l - w + 1), l + 1):
                zl = zp[l2 * nb:(l2 + 1) * nb, cols]
                tot = zl if tot is None else tot + zl
            pieces.append(tot / float(w) - zp[l * nb:(l + 1) * nb, cols])
        diffs.append(jnp.concatenate(pieces, axis=0))
    a_bf = _pool_project(diffs, wpool_ref, pscale_ref[...])

    zuv = _gelu(z[:, pw:])
    u = zuv[:, :pw]
    vn = _head_layernorm(zuv[:, pw:], lng_ref[...], lnb_ref[...], hd)
    vfull = jnp.concatenate(vn, axis=-1)
    for l in range(n_new):
        vn_ref[l] = vfull[l * nb:(l + 1) * nb, :]
    b_cols = []
    for hh in range(SGU_HEADS):
        pieces = []
        for l in range(n_new):
            mixed = None
            for l2 in range(l + 1):
                term = sw_ref[(hh * n_new + l) * n_new + l2] * vn[hh][l2 * nb:(l2 + 1) * nb, :]
                mixed = term if mixed is None else mixed + term
            mixed = mixed + sb_ref[hh * n_new + l]
            pieces.append(u[l * nb:(l + 1) * nb, hh * hd:(hh + 1) * hd] * mixed)
        b_cols.append(jnp.concatenate(pieces, axis=0))
    route(x, a_bf, jnp.concatenate(b_cols, axis=-1).astype(BF16))


def _mixer_body(xp_ref, xs_ref, st_ref, nm_ref, win_ref, wpool_ref, pscale_ref, lng_ref, lnb_ref,
                sw_ref, sbt_ref, sws_ref, sbs_ref, wout_ref, nf_ref, wrt_ref, br_ref,
                x1_ref, h2_ref, idx_ref, w_ref, rank_ref, cnt_out_ref, tail_ref, zp_ref, vn_ref,
                carry_ref, cnt_ref, bo_ref, *, tiles_per_prompt):
    i = pl.program_id(0)
    n_prompt_steps = pl.num_programs(0) - 1

    @pl.when(i == 0)
    def _():
        cnt_ref[...] = jnp.zeros_like(cnt_ref)

    route = functools.partial(_route_and_store, wout_ref=wout_ref, nf_ref=nf_ref, wrt_ref=wrt_ref,
                              br_ref=br_ref, cnt_ref=cnt_ref, x1_ref=x1_ref, h2_ref=h2_ref,
                              idx_ref=idx_ref,
                              w_ref=w_ref, rank_ref=rank_ref, cnt_out_ref=cnt_out_ref)

    @pl.when(i < n_prompt_steps)
    def _():
        _prompt_tile(lax.rem(i, tiles_per_prompt), xp_ref, nm_ref, win_ref, wpool_ref, pscale_ref,
                     lng_ref, lnb_ref, sw_ref, sbt_ref, tail_ref, carry_ref, bo_ref,
                     tiles_per_prompt, route)

    @pl.when(i == n_prompt_steps)
    def _():
        _sample_tile(xs_ref, st_ref, nm_ref, win_ref, wpool_ref, pscale_ref, lng_ref, lnb_ref,
                     sws_ref, sbs_ref, zp_ref, vn_ref, route)


def _row_copy(src, src_row, dst, dst_row, sem):
    s_rows = pl.ds(pl.multiple_of(src_row * SUBLANES, SUBLANES), SUBLANES)
    d_rows = pl.ds(pl.multiple_of(dst_row * SUBLANES, SUBLANES), SUBLANES)
    return pltpu.make_async_copy(src.at[s_rows], dst.at[d_rows], sem)


def _scatter_body(pstart_ref, pcnt_ref, nused_ref, dest_ref, x1_ref, xs_hbm, zero_ref, zsem, rsem):
    i = pl.program_id(0)
    ts = x1_ref.shape[0] // SUBLANES
    tm = zero_ref.shape[0] // SUBLANES
    n_exp = pstart_ref.shape[0]

    def zero_block(row):
        rows = pl.ds(pl.multiple_of(row * SUBLANES, SUBLANES), tm * SUBLANES)
        return pltpu.make_async_copy(zero_ref, xs_hbm.at[rows], zsem)

    @pl.when(i == 0)
    def _():
        zero_ref[...] = jnp.zeros_like(zero_ref)
        for e in range(n_exp):
            zero_block(pstart_ref[e] + pcnt_ref[e]).start()
        for e in range(n_exp):
            zero_block(pstart_ref[e] + pcnt_ref[e]).wait()

        n_total = xs_hbm.shape[0] // (tm * SUBLANES)

        def start_tail(b, c):
            zero_block(b * tm).start()
            return c

        def wait_tail(b, c):
            zero_block(b * tm).wait()
            return c

        lax.fori_loop(nused_ref[0], n_total, start_tail, 0)
        lax.fori_loop(nused_ref[0], n_total, wait_tail, 0)

    def start_rows(r, carry):
        for k in range(TOP_K):
            _row_copy(x1_ref, r, xs_hbm, dest_ref[0, 0, k * ts + r], rsem).start(priority=k % DMA_PRIORITIES)
        return carry

    lax.fori_loop(0, ts, start_rows, 0, unroll=8)

    def wait_rows(r, carry):
        for k in range(TOP_K):
            _row_copy(x1_ref, 0, xs_hbm, 0, rsem).wait()
        return carry

    lax.fori_loop(0, ts, wait_rows, 0, unroll=8)


def _experts_body(be_ref, nused_ref, eord_ref, eorder_ref, ndist_ref,
                  xs_ref, wup_hbm, bup_ref, wdn_hbm, bdn_ref, ys_ref,
                  wup_f32, wdn_f32, wup_bf, wdn_bf, wsem):
    b = pl.program_id(0)
    d_ff = wdn_bf.shape[0]

    def weight_copies(e, slot):
        return (pltpu.make_async_copy(wup_hbm.at[e], wup_f32.at[slot], wsem.at[slot, 0]),
                pltpu.make_async_copy(wdn_hbm.at[e], wdn_f32.at[slot], wsem.at[slot, 1]))

    @pl.when(b == 0)
    def _():
        for cp in weight_copies(eorder_ref[0], 0):
            cp.start()

    @pl.when(b < nused_ref[0])
    def _():
        prev = be_ref[jnp.maximum(b - 1, 0)]

        @pl.when((b == 0) | (be_ref[b] != prev))
        def _():
            j = eord_ref[b]
            slot = lax.rem(j, 2)
            for cp in weight_copies(be_ref[b], slot):
                cp.wait()

            @pl.when(j + 1 < ndist_ref[0])
            def _():
                for cp in weight_copies(eorder_ref[j + 1], 1 - slot):
                    cp.start()

            rows = 128

            def cast_up(r, c):
                sl = pl.ds(pl.multiple_of(r * rows, rows), rows)
                wup_bf[sl, :] = wup_f32[slot, sl, :].astype(BF16)
                return c

            lax.fori_loop(0, wup_bf.shape[0] // rows, cast_up, 0)

            def cast_dn(r, c):
                sl = pl.ds(pl.multiple_of(r * rows, rows), rows)
                wdn_bf[sl, :] = wdn_f32[slot, sl, :].astype(BF16)
                return c

            lax.fori_loop(0, wdn_bf.shape[0] // rows, cast_dn, 0)

        xb = _load_rows(xs_ref, ys_ref.shape[0] // SUBLANES).astype(BF16)
        acc = None
        for j in range(0, d_ff, FF_CHUNK):
            glu = _dot(xb, wup_bf[:, j:j + FF_CHUNK]) + bup_ref[0, :, j:j + FF_CHUNK]
            lin = _dot(xb, wup_bf[:, d_ff + j:d_ff + j + FF_CHUNK]) + bup_ref[0, :, d_ff + j:d_ff + j + FF_CHUNK]
            glu = jnp.minimum(glu, SWIGLU_LIMIT)
            lin = jnp.clip(lin, -SWIGLU_LIMIT, SWIGLU_LIMIT)
            act = glu * jax.nn.sigmoid(SWIGLU_ALPHA * glu) * (lin + 1.0)
            part = _dot(act.astype(BF16), wdn_bf[j:j + FF_CHUNK, :])
            acc = part if acc is None else acc + part
        _store_rows(ys_ref, acc + bdn_ref[0])

    @pl.when(b >= nused_ref[0])
    def _():
        ys_ref[...] = jnp.zeros_like(ys_ref)


def _combine_body(dcur_ref, dnext_ref, wsm_ref, ys_hbm, x1_ref, pp_ref, ps_ref, wg_ref, wp_ref, nfin_ref,
                  yp_ref, ysm_ref, buf, acc_ref, sem, *, n_prompt_tiles):
    i = pl.program_id(0)
    n = pl.num_programs(0)
    ts = x1_ref.shape[0] // SUBLANES
    slot = i % 2

    def issue_row(dref, sl, r):
        for k in range(TOP_K):
            cp = _row_copy(ys_hbm, dref[0, 0, k * ts + r], buf.at[sl, k], r, sem.at[sl])
            cp.start(priority=k % DMA_PRIORITIES)

    @pl.when(i == 0)
    def _():
        def body(r, c):
            issue_row(dcur_ref, 0, r)
            return c
        lax.fori_loop(0, ts, body, 0, unroll=8)

    def wait_rows(r, c):
        for k in range(TOP_K):
            _row_copy(ys_hbm, 0, buf.at[slot, k], 0, sem.at[slot]).wait()
        return c

    lax.fori_loop(0, ts, wait_rows, 0, unroll=8)

    def sum_rows(start_next):
        def body(r, c):
            rows = pl.ds(pl.multiple_of(r * SUBLANES, SUBLANES), SUBLANES)
            acc = x1_ref[rows, :]
            for k in range(TOP_K):
                acc = acc + buf[slot, k, rows, :] * wsm_ref[0, 0, k * ts + r]
            acc_ref[rows, :] = acc
            if start_next:
                issue_row(dnext_ref, 1 - slot, r)
            return c
        lax.fori_loop(0, ts, body, 0, unroll=8)

    @pl.when(i + 1 < n)
    def _():
        sum_rows(True)

    @pl.when(i + 1 == n)
    def _():
        sum_rows(False)

    acc = _load_rows(acc_ref, ts)
    gate = jax.nn.sigmoid(_dot(acc.astype(BF16), wg_ref[...]))
    p = jnp.where(i < n_prompt_tiles, pp_ref[...], ps_ref[...])
    proj = _dot(p.astype(BF16), wp_ref[...])
    y = _rms(acc + gate * proj, nfin_ref[...])

    @pl.when(i < n_prompt_tiles)
    def _():
        yp_ref[...] = y

    @pl.when(i >= n_prompt_tiles)
    def _():
        ysm_ref[...] = y


def _const_spec(shape):
    nd = len(shape)
    return pl.BlockSpec(shape, lambda *_: (0,) * nd)


def kernel(x_prompt, x_sample, state_pool, p_prompt, p_sample, norm_mix, w_in, w_pool, pool_scale,
           sgu_ln_g, sgu_ln_b, sgu_w, sgu_b, w_out, norm_ffn, w_router, b_router, w_up, b_up,
           w_down, b_down, w_ple_gate, w_ple_proj, norm_final):
    B, L, D = x_prompt.shape
    NB, NL, _ = x_sample.shape
    PW = state_pool.shape[-1]
    NE = w_router.shape[-1]
    DFF = w_down.shape[2]
    PLE = p_prompt.shape[-1]
    TP = B * L
    TSM = NB * NL
    T = TP + TSM
    assert w_in.shape[0] == 1, "single trunk layer"
    assert L % TS_MIX == 0 and TSM == TS_MIX and TS_MIX % CHUNK == 0
    assert T % TS_SCATTER == 0 and TP % TS_COMBINE == 0 and TSM % TS_COMBINE == 0
    assert D == SUBLANES * LANES, "a token row must fill exactly one vreg tile"
    assert state_pool.shape[2] == POOL_BUF and PW % len(POOL_WINDOWS) == 0 and PW % SGU_HEADS == 0

    nm = norm_mix[0].reshape(1, D)
    nf = norm_ffn[0].reshape(1, D)
    nfin = norm_final.reshape(1, D)
    win_bf = w_in[0].astype(BF16)
    wpool_bf = w_pool[0].astype(BF16)
    pscale = pool_scale[0].reshape(1, PW)
    lng = sgu_ln_g[0].reshape(1, PW)
    lnb = sgu_ln_b[0].reshape(1, PW)
    wout_bf = w_out[0].astype(BF16)
    wrt = w_router[0].T
    br = b_router[0].reshape(NE, 1)
    wg_bf = w_ple_gate[0].astype(BF16)
    wp_bf = w_ple_proj[0].astype(BF16)

    cparams = functools.partial(pltpu.CompilerParams, vmem_limit_bytes=VMEM_LIMIT_BYTES)
    NT = L // TS_MIX
    n_prompt_steps = B * NT
    xs_rows = x_sample.transpose(1, 0, 2).reshape(TSM, D)
    hist = state_pool[0].transpose(1, 0, 2)
    sw_small = sgu_w[0, :, :NL, :NL].reshape(-1)
    sb_small = sgu_b[0, :, :NL].reshape(-1)
    smem = pl.BlockSpec(memory_space=pltpu.SMEM)
    any_spec = pl.BlockSpec(memory_space=pl.ANY)
    last_prompt = n_prompt_steps - 1
    x1_all, h2_all, idx_t, w_t, rank_t, counts, pool_tail, zp_s, vn_s = pl.pallas_call(
        functools.partial(_mixer_body, tiles_per_prompt=NT),
        grid=(n_prompt_steps + 1,),
        in_specs=[
            pl.BlockSpec((TS_MIX, D), lambda i: (jnp.minimum(i, last_prompt), 0)),
            _const_spec((TSM, D)), _const_spec(hist.shape),
            _const_spec((1, D)), _const_spec(win_bf.shape), _const_spec(wpool_bf.shape),
            _const_spec((1, PW)), _const_spec((1, PW)), _const_spec((1, PW)),
            _const_spec(sgu_w[0].shape), _const_spec((CHUNK, SGU_HEADS)), smem, smem,
            _const_spec(wout_bf.shape), _const_spec((1, D)), _const_spec(wrt.shape),
            _const_spec((NE, 1)),
        ],
        out_specs=[
            pl.BlockSpec((TS_MIX * SUBLANES, LANES), lambda i: (i, 0)),
            pl.BlockSpec((TS_MIX * SUBLANES, LANES), lambda i: (i, 0)),
            pl.BlockSpec((TOP_K, TS_MIX), lambda i: (0, i)),
            pl.BlockSpec((TOP_K, TS_MIX), lambda i: (0, i)),
            pl.BlockSpec((TOP_K, TS_MIX), lambda i: (0, i)),
            _const_spec((NE, 128)),
            pl.BlockSpec((1, CARRY_ROWS, PW), lambda i: (jnp.minimum(i, last_prompt) // NT, 0, 0)),
            _const_spec((NL, NB, PW)),
            _const_spec((NL, NB, PW)),
        ],
        out_shape=[
            jax.ShapeDtypeStruct((T * SUBLANES, LANES), F32),
            jax.ShapeDtypeStruct((T * SUBLANES, LANES), F32),
            jax.ShapeDtypeStruct((TOP_K, T), I32),
            jax.ShapeDtypeStruct((TOP_K, T), F32),
            jax.ShapeDtypeStruct((TOP_K, T), I32),
            jax.ShapeDtypeStruct((NE, 128), F32),
            jax.ShapeDtypeStruct((B, CARRY_ROWS, PW), F32),
            jax.ShapeDtypeStruct((NL, NB, PW), F32),
            jax.ShapeDtypeStruct((NL, NB, PW), F32),
        ],
        scratch_shapes=[pltpu.VMEM((CARRY_ROWS, PW), F32), pltpu.VMEM((NE, 128), F32),
                        pltpu.VMEM((TS_MIX, PW), BF16)],
        compiler_params=cparams(dimension_semantics=("arbitrary",)),
        name="mixer",
    )(x_prompt.reshape(TP, D), xs_rows, hist, nm, win_bf, wpool_bf, pscale, lng, lnb, sgu_w[0],
      sgu_b[0].T, sw_small, sb_small, wout_bf, nf, wrt, br)

    A = T * TOP_K
    n_blocks = -(-A // TM_MOE) + NE
    n_slots = n_blocks * TM_MOE
    cnt_i = counts[:, 0].astype(I32)
    padded = (cnt_i + TM_MOE - 1) // TM_MOE * TM_MOE
    pad_end = jnp.cumsum(padded)
    pad_start = pad_end - padded
    n_used = (pad_end[-1] // TM_MOE).astype(I32).reshape(1)
    blk = jnp.minimum(jnp.arange(n_blocks, dtype=I32), n_used[0] - 1)
    block_e = jnp.minimum(jnp.sum(pad_end[None, :] <= (blk * TM_MOE)[:, None], axis=1), NE - 1).astype(I32)
    expert_ids = jnp.arange(NE, dtype=I32)[:, None, None]
    dest = jnp.sum(jnp.where(idx_t[None] == expert_ids, pad_start[:, None, None], 0), axis=0) + rank_t
    used = padded > 0
    ordinal = jnp.cumsum(used.astype(I32)) - 1
    n_distinct = jnp.sum(used.astype(I32)).reshape(1)
    block_ord = jnp.sum(jnp.where(block_e[:, None] == expert_ids[:, 0, 0][None, :], ordinal[None, :], 0), axis=1)
    slots_j = jnp.minimum(jnp.arange(NE + 1, dtype=I32), n_distinct[0] - 1)
    expert_order = jnp.sum(jnp.where(used[None, :] & (ordinal[None, :] == slots_j[:, None]),
                                     expert_ids[:, 0, 0][None, :], 0), axis=1).astype(I32)

    def per_tile(a, ts):
        return a.reshape(TOP_K, T // ts, ts).transpose(1, 0, 2).reshape(T // ts, 1, TOP_K * ts)

    xs_sorted = pl.pallas_call(
        _scatter_body,
        grid_spec=pltpu.PrefetchScalarGridSpec(
            num_scalar_prefetch=3,
            grid=(T // TS_SCATTER,),
            in_specs=[pl.BlockSpec((1, 1, TOP_K * TS_SCATTER), lambda i, *_: (i, 0, 0),
                                   memory_space=pltpu.SMEM),
                      pl.BlockSpec((TS_SCATTER * SUBLANES, LANES), lambda i, *_: (i, 0))],
            out_specs=any_spec,
            scratch_shapes=[pltpu.VMEM((TM_MOE * SUBLANES, LANES), F32), pltpu.SemaphoreType.DMA(()),
                            pltpu.SemaphoreType.DMA(())],
        ),
        out_shape=jax.ShapeDtypeStruct(((n_slots + TM_MOE) * SUBLANES, LANES), F32),
        compiler_params=cparams(dimension_semantics=("arbitrary",)),
        name="moe_scatter",
    )(pad_start, cnt_i, n_used, per_tile(dest, TS_SCATTER), h2_all)

    ys_sorted = pl.pallas_call(
        _experts_body,
        grid_spec=pltpu.PrefetchScalarGridSpec(
            num_scalar_prefetch=5,
            grid=(n_blocks,),
            in_specs=[
                pl.BlockSpec((TM_MOE * SUBLANES, LANES), lambda b, be, nu, *_: (jnp.minimum(b, nu[0] - 1), 0)),
                any_spec,
                pl.BlockSpec((1, 1, 2 * DFF), lambda b, be, *_: (be[b], 0, 0)),
                any_spec,
                pl.BlockSpec((1, 1, D), lambda b, be, *_: (be[b], 0, 0)),
            ],
            out_specs=pl.BlockSpec((TM_MOE * SUBLANES, LANES), lambda b, *_: (b, 0)),
            scratch_shapes=[pltpu.VMEM((2, D, 2 * DFF), F32), pltpu.VMEM((2, DFF, D), F32),
                            pltpu.VMEM((D, 2 * DFF), BF16), pltpu.VMEM((DFF, D), BF16),
                            pltpu.SemaphoreType.DMA((2, 2))],
        ),
        out_shape=jax.ShapeDtypeStruct((n_slots * SUBLANES, LANES), F32),
        compiler_params=cparams(dimension_semantics=("arbitrary",)),
        name="moe_experts",
    )(block_e, n_used, block_ord.astype(I32), expert_order, n_distinct, xs_sorted, w_up[0],
      b_up[0].reshape(NE, 1, 2 * DFF), w_down[0], b_down[0].reshape(NE, 1, D))

    n_tiles = T // TS_COMBINE
    npt = TP // TS_COMBINE
    y_p, y_s = pl.pallas_call(
        functools.partial(_combine_body, n_prompt_tiles=npt),
        grid=(n_tiles,),
        in_specs=[
            pl.BlockSpec((1, 1, TOP_K * TS_COMBINE), lambda i: (i, 0, 0), memory_space=pltpu.SMEM),
            pl.BlockSpec((1, 1, TOP_K * TS_COMBINE), lambda i: (jnp.minimum(i + 1, n_tiles - 1), 0, 0),
                         memory_space=pltpu.SMEM),
            pl.BlockSpec((1, 1, TOP_K * TS_COMBINE), lambda i: (i, 0, 0), memory_space=pltpu.SMEM),
            any_spec,
            pl.BlockSpec((TS_COMBINE * SUBLANES, LANES), lambda i: (i, 0)),
            pl.BlockSpec((TS_COMBINE, PLE), lambda i: (jnp.minimum(i, npt - 1), 0)),
            pl.BlockSpec((TS_COMBINE, PLE), lambda i: (jnp.maximum(i - npt, 0), 0)),
            _const_spec(wg_bf.shape), _const_spec(wp_bf.shape), _const_spec((1, D)),
        ],
        out_specs=[
            pl.BlockSpec((TS_COMBINE, D), lambda i: (jnp.minimum(i, npt - 1), 0)),
            pl.BlockSpec((TS_COMBINE, D), lambda i: (jnp.maximum(i - npt, 0), 0)),
        ],
        out_shape=[jax.ShapeDtypeStruct((TP, D), F32), jax.ShapeDtypeStruct((TSM, D), F32)],
        scratch_shapes=[pltpu.VMEM((2, TOP_K, TS_COMBINE * SUBLANES, LANES), F32),
                        pltpu.VMEM((TS_COMBINE * SUBLANES, LANES), F32),
                        pltpu.SemaphoreType.DMA((2,))],
        compiler_params=cparams(dimension_semantics=("arbitrary",)),
        name="moe_combine",
    )(per_tile(dest, TS_COMBINE), per_tile(dest, TS_COMBINE), per_tile(w_t, TS_COMBINE), ys_sorted,
      x1_all, p_prompt[0].reshape(TP, PLE), p_sample[0].transpose(1, 0, 2).reshape(TSM, PLE),
      wg_bf, wp_bf, nfin)

    y_prompt = y_p.reshape(B, L, D)
    y_sample = y_s.reshape(NL, NB, D).transpose(1, 0, 2)
    new_pool_prompt = pool_tail[:, CARRY_ROWS - POOL_BUF:, :][None]
    new_pool_sample = jnp.concatenate([state_pool[0][:, NL:, :], zp_s.transpose(1, 0, 2)], axis=1)[None]
    new_sgu_v_sample = vn_s.transpose(1, 0, 2)[None]
    return (y_prompt, y_sample, new_pool_prompt, new_pool_sample, new_sgu_v_sample)
```

```python
import functools

import jax
import jax.numpy as jnp
from jax import lax
from jax.experimental import pallas as pl
from jax.experimental.pallas import tpu as pltpu

F32 = jnp.float32
BF16 = jnp.bfloat16
I32 = jnp.int32

POOL_WINDOWS = (2, 4, 8, 16)
POOL_BUF = max(POOL_WINDOWS) - 1
SGU_HEADS = 4
CHUNK = 128
TOP_K = 4
SWIGLU_LIMIT = 7.0
SWIGLU_ALPHA = 1.702
RMS_EPS = 1e-6
LN_EPS = 1e-5

SUBLANES = 8
LANES = 128
CARRY_ROWS = 16
TS_MIX = 512
TS_SCATTER = 512
DMA_PRIORITIES = 2
TM_MOE = 512
FF_CHUNK = 512
ROW_VARIANTS = 4
TS_COMBINE = 512
VMEM_LIMIT_BYTES = 56 * 1024 * 1024


def _rms(x, g):
    return x * lax.rsqrt(jnp.mean(x * x, axis=-1, keepdims=True) + RMS_EPS) * g


def _dot(a, b):
    return jnp.dot(a, b, preferred_element_type=F32)


def _load_rows(ref, n_rows):
    return jnp.concatenate([ref[pl.ds(c, n_rows, stride=SUBLANES), :] for c in range(SUBLANES)], axis=-1)


def _store_rows(ref, val):
    n_rows = val.shape[0]
    for c in range(SUBLANES):
        ref[pl.ds(c, n_rows, stride=SUBLANES), :] = val[:, c * LANES:(c + 1) * LANES]


def _gelu(x):
    return 0.5 * x * (1.0 + lax.erf(x * (2.0 ** -0.5)))


def _head_layernorm(v, g, b, hd):
    outs = []
    for h in range(SGU_HEADS):
        vh = v[:, h * hd:(h + 1) * hd]
        mu = jnp.mean(vh, axis=-1, keepdims=True)
        d = vh - mu
        var = jnp.mean(d * d, axis=-1, keepdims=True)
        outs.append(d * lax.rsqrt(var + LN_EPS) * g[:, h * hd:(h + 1) * hd] + b[:, h * hd:(h + 1) * hd])
    return outs


def _pool_project(diffs, wpool_ref, pscale):
    gd = diffs[0].shape[-1]
    outs = []
    for g, d in enumerate(diffs):
        outs.append(_dot(d.astype(BF16), wpool_ref[g]) * pscale[:, g * gd:(g + 1) * gd])
    return jnp.concatenate(outs, axis=-1).astype(BF16)


def _route_and_store(x, a_bf, b_bf, wout_ref, nf_ref, wrt_ref, br_ref, cnt_ref,
                     x1_ref, h2_ref, idx_ref, w_ref, rank_ref, cnt_out_ref):
    ts = x.shape[0]
    pw = a_bf.shape[-1]
    n_exp = wrt_ref.shape[0]
    mix = _dot(a_bf, wout_ref[:pw, :]) + _dot(b_bf, wout_ref[pw:, :])
    x1 = x + mix
    _store_rows(x1_ref, x1)
    h2 = _rms(x1, nf_ref[...])
    _store_rows(h2_ref, h2)
    logits = lax.dot_general(wrt_ref[...], h2, (((1,), (1,)), ((), ())),
                             precision=lax.Precision.HIGHEST,
                             preferred_element_type=F32) + br_ref[...]
    eio = lax.broadcasted_iota(I32, (n_exp, ts), 0).astype(F32)
    l = logits
    tops, sels = [], []
    for _ in range(TOP_K):
        m = jnp.max(l, axis=0, keepdims=True)
        sel = jnp.min(jnp.where(l == m, eio, float(n_exp)), axis=0, keepdims=True)
        tops.append(m)
        sels.append(sel)
        l = jnp.where(eio == sel, -jnp.inf, l)
    exps = [jnp.exp(t - tops[0]) for t in tops]
    denom = exps[0] + exps[1] + exps[2] + exps[3]
    onehot = jnp.zeros((n_exp, ts), F32)
    for sel in sels:
        onehot = onehot + jnp.where(eio == sel, 1.0, 0.0)
    r_i = lax.broadcasted_iota(I32, (ts, ts), 0)
    c_i = lax.broadcasted_iota(I32, (ts, ts), 1)
    upper = jnp.where(r_i < c_i, 1.0, 0.0).astype(BF16)
    base = cnt_ref[:, 0:1]
    rank_all = _dot(onehot.astype(BF16), upper) + base
    for k in range(TOP_K):
        idx_ref[k:k + 1, :] = sels[k].astype(I32)
        w_ref[k:k + 1, :] = exps[k] / denom
        rk = jnp.sum(jnp.where(eio == sels[k], rank_all, 0.0), axis=0, keepdims=True)
        rank_ref[k:k + 1, :] = rk.astype(I32)
    cnt_ref[...] = cnt_ref[...] + jnp.sum(onehot, axis=1, keepdims=True)
    cnt_out_ref[...] = cnt_ref[...]


def _prompt_tile(t, x_ref, nm_ref, win_ref, wpool_ref, pscale_ref, lng_ref, lnb_ref, sw_ref, sbt_ref,
                 tail_ref, carry_ref, bo_ref, n_tiles, route):
    ts = x_ref.shape[0]
    pw = pscale_ref.shape[-1]
    gd = pw // len(POOL_WINDOWS)
    hd = pw // SGU_HEADS

    @pl.when(t == 0)
    def _():
        carry_ref[...] = jnp.zeros_like(carry_ref)

    x = x_ref[...]
    h = _rms(x, nm_ref[...])
    z = _dot(h.astype(BF16), win_ref[...])
    zp = z[:, :pw]

    ext = jnp.concatenate([carry_ref[...], zp], axis=0)
    carry_ref[...] = zp[ts - CARRY_ROWS:, :]

    @pl.when(t == n_tiles - 1)
    def _():
        tail_ref[0] = zp[ts - CARRY_ROWS:, :]

    pos = t * ts + lax.broadcasted_iota(I32, (ts, 1), 0)
    diffs = []
    for g, w in enumerate(POOL_WINDOWS):
        acc = ext[:, g * gd:(g + 1) * gd]
        k = 1
        while k < w:
            acc = acc + pltpu.roll(acc, k, 0)
            k *= 2
        cnt = jnp.minimum(w, pos + 1).astype(F32)
        diffs.append(acc[CARRY_ROWS:, :] / cnt - zp[:, g * gd:(g + 1) * gd])
    a_bf = _pool_project(diffs, wpool_ref, pscale_ref[...])

    zuv = _gelu(z[:, pw:])
    u = zuv[:, :pw]
    vn = _head_layernorm(zuv[:, pw:], lng_ref[...], lnb_ref[...], hd)
    ri = lax.broadcasted_iota(I32, (CHUNK, CHUNK), 0)
    ci = lax.broadcasted_iota(I32, (CHUNK, CHUNK), 1)
    for hh in range(SGU_HEADS):
        wm = jnp.where(ci <= ri, sw_ref[hh], 0.0).astype(BF16)
        bias = sbt_ref[:, hh:hh + 1]
        vh = vn[hh].astype(BF16)
        for c in range(ts // CHUNK):
            rows = slice(c * CHUNK, (c + 1) * CHUNK)
            mixed = _dot(wm, vh[rows, :]) + bias
            bo_ref[rows, hh * hd:(hh + 1) * hd] = (u[rows, hh * hd:(hh + 1) * hd] * mixed).astype(BF16)
    route(x, a_bf, bo_ref[...])


def _sample_tile(x_ref, st_ref, nm_ref, win_ref, wpool_ref, pscale_ref, lng_ref, lnb_ref, sw_ref,
                 sb_ref, zp_ref, vn_ref, route):
    n_new, nb, pw = zp_ref.shape
    gd = pw // len(POOL_WINDOWS)
    hd = pw // SGU_HEADS

    x = x_ref[...]
    h = _rms(x, nm_ref[...])
    z = _dot(h.astype(BF16), win_ref[...])
    zp = z[:, :pw]
    for l in range(n_new):
        zp_ref[l] = zp[l * nb:(l + 1) * nb, :]

    diffs = []
    for g, w in enumerate(POOL_WINDOWS):
        cols = slice(g * gd, (g + 1) * gd)
        lo_needed = POOL_BUF + 1 - w
        suffix = {}
        run = None
        for r in range(POOL_BUF - 1, lo_needed - 1, -1):
            slab = st_ref[r][:, cols]
            run = slab if run is None else run + slab
            suffix[r] = run
        pieces = []
        for l in range(n_new):
            first_hist = POOL_BUF + 1 + l - w
            tot = suffix[first_hist] if first_hist < POOL_BUF else None
            for l2 in range(max(0, l - w + 1), l + 1):
                zl = zp[l2 * nb:(l2 + 1) * nb, cols]
                tot = zl if tot is None else tot + zl
            pieces.append(tot / float(w) - zp[l * nb:(l + 1) * nb, cols])
        diffs.append(jnp.concatenate(pieces, axis=0))
    a_bf = _pool_project(diffs, wpool_ref, pscale_ref[...])

    zuv = _gelu(z[:, pw:])
    u = zuv[:, :pw]
    vn = _head_layernorm(zuv[:, pw:], lng_ref[...], lnb_ref[...], hd)
    vfull = jnp.concatenate(vn, axis=-1)
    for l in range(n_new):
        vn_ref[l] = vfull[l * nb:(l + 1) * nb, :]
    b_cols = []
    for hh in range(SGU_HEADS):
        pieces = []
        for l in range(n_new):
            mixed = None
            for l2 in range(l + 1):
                term = sw_ref[(hh * n_new + l) * n_new + l2] * vn[hh][l2 * nb:(l2 + 1) * nb, :]
                mixed = term if mixed is None else mixed + term
            mixed = mixed + sb_ref[hh * n_new + l]
            pieces.append(u[l * nb:(l + 1) * nb, hh * hd:(hh + 1) * hd] * mixed)
        b_cols.append(jnp.concatenate(pieces, axis=0))
    route(x, a_bf, jnp.concatenate(b_cols, axis=-1).astype(BF16))


def _mixer_body(xp_ref, xs_ref, st_ref, nm_ref, win_ref, wpool_ref, pscale_ref, lng_ref, lnb_ref,
                sw_ref, sbt_ref, sws_ref, sbs_ref, wout_ref, nf_ref, wrt_ref, br_ref,
                x1_ref, h2_ref, idx_ref, w_ref, rank_ref, cnt_out_ref, tail_ref, zp_ref, vn_ref,
                carry_ref, cnt_ref, bo_ref, *, tiles_per_prompt):
    i = pl.program_id(0)
    n_prompt_steps = pl.num_programs(0) - 1

    @pl.when(i == 0)
    def _():
        cnt_ref[...] = jnp.zeros_like(cnt_ref)

    route = functools.partial(_route_and_store, wout_ref=wout_ref, nf_ref=nf_ref, wrt_ref=wrt_ref,
                              br_ref=br_ref, cnt_ref=cnt_ref, x1_ref=x1_ref, h2_ref=h2_ref,
                              idx_ref=idx_ref,
                              w_ref=w_ref, rank_ref=rank_ref, cnt_out_ref=cnt_out_ref)

    @pl.when(i < n_prompt_steps)
    def _():
        _prompt_tile(lax.rem(i, tiles_per_prompt), xp_ref, nm_ref, win_ref, wpool_ref, pscale_ref,
                     lng_ref, lnb_ref, sw_ref, sbt_ref, tail_ref, carry_ref, bo_ref,
                     tiles_per_prompt, route)

    @pl.when(i == n_prompt_steps)
    def _():
        _sample_tile(xs_ref, st_ref, nm_ref, win_ref, wpool_ref, pscale_ref, lng_ref, lnb_ref,
                     sws_ref, sbs_ref, zp_ref, vn_ref, route)


def _row_copy(src, src_row, dst, dst_row, sem):
    s_rows = pl.ds(pl.multiple_of(src_row * SUBLANES, SUBLANES), SUBLANES)
    d_rows = pl.ds(pl.multiple_of(dst_row * SUBLANES, SUBLANES), SUBLANES)
    return pltpu.make_async_copy(src.at[s_rows], dst.at[d_rows], sem)


def _scatter_body(pstart_ref, pcnt_ref, nused_ref, dest_ref, x1_ref, xs_hbm, zero_ref, zsem, rsem):
    i = pl.program_id(0)
    ts = x1_ref.shape[0] // SUBLANES
    tm = zero_ref.shape[0] // SUBLANES
    n_exp = pstart_ref.shape[0]

    def zero_block(row):
        rows = pl.ds(pl.multiple_of(row * SUBLANES, SUBLANES), tm * SUBLANES)
        return pltpu.make_async_copy(zero_ref, xs_hbm.at[rows], zsem)

    @pl.when(i == 0)
    def _():
        zero_ref[...] = jnp.zeros_like(zero_ref)
        for e in range(n_exp):
            zero_block(pstart_ref[e] + pcnt_ref[e]).start()
        for e in range(n_exp):
            zero_block(pstart_ref[e] + pcnt_ref[e]).wait()

        n_total = xs_hbm.shape[0] // (tm * SUBLANES)

        def start_tail(b, c):
            zero_block(b * tm).start()
            return c

        def wait_tail(b, c):
            zero_block(b * tm).wait()
            return c

        lax.fori_loop(nused_ref[0], n_total, start_tail, 0)
        lax.fori_loop(nused_ref[0], n_total, wait_tail, 0)

    def start_rows(r, carry):
        for k in range(TOP_K):
            _row_copy(x1_ref, r, xs_hbm, dest_ref[0, 0, k * ts + r], rsem).start(priority=k % DMA_PRIORITIES)
        return carry

    lax.fori_loop(0, ts, start_rows, 0, unroll=8)

    def wait_rows(r, carry):
        for k in range(TOP_K):
            _row_copy(x1_ref, 0, xs_hbm, 0, rsem).wait()
        return carry

    lax.fori_loop(0, ts, wait_rows, 0, unroll=8)


def _experts_body(be_ref, nused_ref, eord_ref, eorder_ref, ndist_ref, bvalid_ref,
                  xs_ref, wup_hbm, bup_ref, wdn_hbm, bdn_ref, ys_ref,
                  wup_f32, wdn_f32, wup_bf, wdn_bf, wsem):
    b = pl.program_id(0)
    d_ff = wdn_bf.shape[0]

    def weight_copies(e, slot):
        return (pltpu.make_async_copy(wup_hbm.at[e], wup_f32.at[slot], wsem.at[slot, 0]),
                pltpu.make_async_copy(wdn_hbm.at[e], wdn_f32.at[slot], wsem.at[slot, 1]))

    @pl.when(b == 0)
    def _():
        for cp in weight_copies(eorder_ref[0], 0):
            cp.start()

    @pl.when(b < nused_ref[0])
    def _():
        prev = be_ref[jnp.maximum(b - 1, 0)]

        @pl.when((b == 0) | (be_ref[b] != prev))
        def _():
            j = eord_ref[b]
            slot = lax.rem(j, 2)
            for cp in weight_copies(be_ref[b], slot):
                cp.wait()

            @pl.when(j + 1 < ndist_ref[0])
            def _():
                for cp in weight_copies(eorder_ref[j + 1], 1 - slot):
                    cp.start()

            rows = 128

            def cast_up(r, c):
                sl = pl.ds(pl.multiple_of(r * rows, rows), rows)
                wup_bf[sl, :] = wup_f32[slot, sl, :].astype(BF16)
                return c

            lax.fori_loop(0, wup_bf.shape[0] // rows, cast_up, 0)

            def cast_dn(r, c):
                sl = pl.ds(pl.multiple_of(r * rows, rows), rows)
                wdn_bf[sl, :] = wdn_f32[slot, sl, :].astype(BF16)
                return c

            lax.fori_loop(0, wdn_bf.shape[0] // rows, cast_dn, 0)

        def compute(n_rows):
            xb = _load_rows(xs_ref, n_rows).astype(BF16)
            acc = None
            for j in range(0, d_ff, FF_CHUNK):
                glu = _dot(xb, wup_bf[:, j:j + FF_CHUNK]) + bup_ref[0, :, j:j + FF_CHUNK]
                lin = (_dot(xb, wup_bf[:, d_ff + j:d_ff + j + FF_CHUNK])
                       + bup_ref[0, :, d_ff + j:d_ff + j + FF_CHUNK])
                glu = jnp.minimum(glu, SWIGLU_LIMIT)
                lin = jnp.clip(lin, -SWIGLU_LIMIT, SWIGLU_LIMIT)
                act = glu * jax.nn.sigmoid(SWIGLU_ALPHA * glu) * (lin + 1.0)
                part = _dot(act.astype(BF16), wdn_bf[j:j + FF_CHUNK, :])
                acc = part if acc is None else acc + part
            _store_rows(ys_ref, acc + bdn_ref[0])
            if n_rows * SUBLANES < ys_ref.shape[0]:
                ys_ref[n_rows * SUBLANES:, :] = jnp.zeros(
                    (ys_ref.shape[0] - n_rows * SUBLANES, ys_ref.shape[1]), ys_ref.dtype)

        tm = ys_ref.shape[0] // SUBLANES
        step = tm // ROW_VARIANTS
        valid = bvalid_ref[b]
        for hi in range(step, tm + 1, step):
            @pl.when((valid > (hi - step if hi > step else -1)) & (valid <= hi))
            def _(hi=hi):
                compute(hi)

    @pl.when(b >= nused_ref[0])
    def _():
        ys_ref[...] = jnp.zeros_like(ys_ref)


def _combine_body(dcur_ref, dnext_ref, wsm_ref, ys_hbm, x1_ref, pp_ref, ps_ref, wg_ref, wp_ref, nfin_ref,
                  yp_ref, ysm_ref, buf, acc_ref, sem, *, n_prompt_tiles):
    i = pl.program_id(0)
    n = pl.num_programs(0)
    ts = x1_ref.shape[0] // SUBLANES
    slot = i % 2

    def issue_row(dref, sl, r):
        for k in range(TOP_K):
            cp = _row_copy(ys_hbm, dref[0, 0, k * ts + r], buf.at[sl, k], r, sem.at[sl])
            cp.start(priority=k % DMA_PRIORITIES)

    @pl.when(i == 0)
    def _():
        def body(r, c):
            issue_row(dcur_ref, 0, r)
            return c
        lax.fori_loop(0, ts, body, 0, unroll=8)

    def wait_rows(r, c):
        for k in range(TOP_K):
            _row_copy(ys_hbm, 0, buf.at[slot, k], 0, sem.at[slot]).wait()
        return c

    lax.fori_loop(0, ts, wait_rows, 0, unroll=8)

    def sum_rows(start_next):
        def body(r, c):
            rows = pl.ds(pl.multiple_of(r * SUBLANES, SUBLANES), SUBLANES)
            acc = x1_ref[rows, :]
            for k in range(TOP_K):
                acc = acc + buf[slot, k, rows, :] * wsm_ref[0, 0, k * ts + r]
            acc_ref[rows, :] = acc
            if start_next:
                issue_row(dnext_ref, 1 - slot, r)
            return c
        lax.fori_loop(0, ts, body, 0, unroll=8)

    @pl.when(i + 1 < n)
    def _():
        sum_rows(True)

    @pl.when(i + 1 == n)
    def _():
        sum_rows(False)

    acc = _load_rows(acc_ref, ts)
    gate = jax.nn.sigmoid(_dot(acc.astype(BF16), wg_ref[...]))
    p = jnp.where(i < n_prompt_tiles, pp_ref[...], ps_ref[...])
    proj = _dot(p.astype(BF16), wp_ref[...])
    y = _rms(acc + gate * proj, nfin_ref[...])

    @pl.when(i < n_prompt_tiles)
    def _():
        yp_ref[...] = y

    @pl.when(i >= n_prompt_tiles)
    def _():
        ysm_ref[...] = y


def _const_spec(shape):
    nd = len(shape)
    return pl.BlockSpec(shape, lambda *_: (0,) * nd)


def kernel(x_prompt, x_sample, state_pool, p_prompt, p_sample, norm_mix, w_in, w_pool, pool_scale,
           sgu_ln_g, sgu_ln_b, sgu_w, sgu_b, w_out, norm_ffn, w_router, b_router, w_up, b_up,
           w_down, b_down, w_ple_gate, w_ple_proj, norm_final):
    B, L, D = x_prompt.shape
    NB, NL, _ = x_sample.shape
    PW = state_pool.shape[-1]
    NE = w_router.shape[-1]
    DFF = w_down.shape[2]
    PLE = p_prompt.shape[-1]
    TP = B * L
    TSM = NB * NL
    T = TP + TSM
    assert w_in.shape[0] == 1, "single trunk layer"
    assert L % TS_MIX == 0 and TSM == TS_MIX and TS_MIX % CHUNK == 0
    assert T % TS_SCATTER == 0 and TP % TS_COMBINE == 0 and TSM % TS_COMBINE == 0
    assert D == SUBLANES * LANES, "a token row must fill exactly one vreg tile"
    assert state_pool.shape[2] == POOL_BUF and PW % len(POOL_WINDOWS) == 0 and PW % SGU_HEADS == 0

    nm = norm_mix[0].reshape(1, D)
    nf = norm_ffn[0].reshape(1, D)
    nfin = norm_final.reshape(1, D)
    win_bf = w_in[0].astype(BF16)
    wpool_bf = w_pool[0].astype(BF16)
    pscale = pool_scale[0].reshape(1, PW)
    lng = sgu_ln_g[0].reshape(1, PW)
    lnb = sgu_ln_b[0].reshape(1, PW)
    wout_bf = w_out[0].astype(BF16)
    wrt = w_router[0].T
    br = b_router[0].reshape(NE, 1)
    wg_bf = w_ple_gate[0].astype(BF16)
    wp_bf = w_ple_proj[0].astype(BF16)

    cparams = functools.partial(pltpu.CompilerParams, vmem_limit_bytes=VMEM_LIMIT_BYTES)
    NT = L // TS_MIX
    n_prompt_steps = B * NT
    xs_rows = x_sample.transpose(1, 0, 2).reshape(TSM, D)
    hist = state_pool[0].transpose(1, 0, 2)
    sw_small = sgu_w[0, :, :NL, :NL].reshape(-1)
    sb_small = sgu_b[0, :, :NL].reshape(-1)
    smem = pl.BlockSpec(memory_space=pltpu.SMEM)
    any_spec = pl.BlockSpec(memory_space=pl.ANY)
    last_prompt = n_prompt_steps - 1
    x1_all, h2_all, idx_t, w_t, rank_t, counts, pool_tail, zp_s, vn_s = pl.pallas_call(
        functools.partial(_mixer_body, tiles_per_prompt=NT),
        grid=(n_prompt_steps + 1,),
        in_specs=[
            pl.BlockSpec((TS_MIX, D), lambda i: (jnp.minimum(i, last_prompt), 0)),
            _const_spec((TSM, D)), _const_spec(hist.shape),
            _const_spec((1, D)), _const_spec(win_bf.shape), _const_spec(wpool_bf.shape),
            _const_spec((1, PW)), _const_spec((1, PW)), _const_spec((1, PW)),
            _const_spec(sgu_w[0].shape), _const_spec((CHUNK, SGU_HEADS)), smem, smem,
            _const_spec(wout_bf.shape), _const_spec((1, D)), _const_spec(wrt.shape),
            _const_spec((NE, 1)),
        ],
        out_specs=[
            pl.BlockSpec((TS_MIX * SUBLANES, LANES), lambda i: (i, 0)),
            pl.BlockSpec((TS_MIX * SUBLANES, LANES), lambda i: (i, 0)),
            pl.BlockSpec((TOP_K, TS_MIX), lambda i: (0, i)),
            pl.BlockSpec((TOP_K, TS_MIX), lambda i: (0, i)),
            pl.BlockSpec((TOP_K, TS_MIX), lambda i: (0, i)),
            _const_spec((NE, 128)),
            pl.BlockSpec((1, CARRY_ROWS, PW), lambda i: (jnp.minimum(i, last_prompt) // NT, 0, 0)),
            _const_spec((NL, NB, PW)),
            _const_spec((NL, NB, PW)),
        ],
        out_shape=[
            jax.ShapeDtypeStruct((T * SUBLANES, LANES), F32),
            jax.ShapeDtypeStruct((T * SUBLANES, LANES), F32),
            jax.ShapeDtypeStruct((TOP_K, T), I32),
            jax.ShapeDtypeStruct((TOP_K, T), F32),
            jax.ShapeDtypeStruct((TOP_K, T), I32),
            jax.ShapeDtypeStruct((NE, 128), F32),
            jax.ShapeDtypeStruct((B, CARRY_ROWS, PW), F32),
            jax.ShapeDtypeStruct((NL, NB, PW), F32),
            jax.ShapeDtypeStruct((NL, NB, PW), F32),
        ],
        scratch_shapes=[pltpu.VMEM((CARRY_ROWS, PW), F32), pltpu.VMEM((NE, 128), F32),
                        pltpu.VMEM((TS_MIX, PW), BF16)],
        compiler_params=cparams(dimension_semantics=("arbitrary",)),
        name="mixer",
    )(x_prompt.reshape(TP, D), xs_rows, hist, nm, win_bf, wpool_bf, pscale, lng, lnb, sgu_w[0],
      sgu_b[0].T, sw_small, sb_small, wout_bf, nf, wrt, br)

    A = T * TOP_K
    n_blocks = -(-A // TM_MOE) + NE
    n_slots = n_blocks * TM_MOE
    cnt_i = counts[:, 0].astype(I32)
    padded = (cnt_i + TM_MOE - 1) // TM_MOE * TM_MOE
    pad_end = jnp.cumsum(padded)
    pad_start = pad_end - padded
    n_used = (pad_end[-1] // TM_MOE).astype(I32).reshape(1)
    blk = jnp.minimum(jnp.arange(n_blocks, dtype=I32), n_used[0] - 1)
    block_e = jnp.minimum(jnp.sum(pad_end[None, :] <= (blk * TM_MOE)[:, None], axis=1), NE - 1).astype(I32)
    expert_ids = jnp.arange(NE, dtype=I32)[:, None, None]
    dest = jnp.sum(jnp.where(idx_t[None] == expert_ids, pad_start[:, None, None], 0), axis=0) + rank_t
    used = padded > 0
    ordinal = jnp.cumsum(used.astype(I32)) - 1
    n_distinct = jnp.sum(used.astype(I32)).reshape(1)
    block_ord = jnp.sum(jnp.where(block_e[:, None] == expert_ids[:, 0, 0][None, :], ordinal[None, :], 0), axis=1)
    slots_j = jnp.minimum(jnp.arange(NE + 1, dtype=I32), n_distinct[0] - 1)
    expert_order = jnp.sum(jnp.where(used[None, :] & (ordinal[None, :] == slots_j[:, None]),
                                     expert_ids[:, 0, 0][None, :], 0), axis=1).astype(I32)

    onehot_be = block_e[:, None] == expert_ids[:, 0, 0][None, :]
    first_row = jnp.sum(jnp.where(onehot_be, pad_start[None, :], 0), axis=1)
    rows_left = jnp.sum(jnp.where(onehot_be, cnt_i[None, :], 0), axis=1) - (blk * TM_MOE - first_row)
    block_valid = jnp.where(jnp.arange(n_blocks, dtype=I32) < n_used[0],
                            jnp.clip(rows_left, 0, TM_MOE), 0).astype(I32)

    def per_tile(a, ts):
        return a.reshape(TOP_K, T // ts, ts).transpose(1, 0, 2).reshape(T // ts, 1, TOP_K * ts)

    xs_sorted = pl.pallas_call(
        _scatter_body,
        grid_spec=pltpu.PrefetchScalarGridSpec(
            num_scalar_prefetch=3,
            grid=(T // TS_SCATTER,),
            in_specs=[pl.BlockSpec((1, 1, TOP_K * TS_SCATTER), lambda i, *_: (i, 0, 0),
                                   memory_space=pltpu.SMEM),
                      pl.BlockSpec((TS_SCATTER * SUBLANES, LANES), lambda i, *_: (i, 0))],
            out_specs=any_spec,
            scratch_shapes=[pltpu.VMEM((TM_MOE * SUBLANES, LANES), F32), pltpu.SemaphoreType.DMA(()),
                            pltpu.SemaphoreType.DMA(())],
        ),
        out_shape=jax.ShapeDtypeStruct(((n_slots + TM_MOE) * SUBLANES, LANES), F32),
        compiler_params=cparams(dimension_semantics=("arbitrary",)),
        name="moe_scatter",
    )(pad_start, cnt_i, n_used, per_tile(dest, TS_SCATTER), h2_all)

    ys_sorted = pl.pallas_call(
        _experts_body,
        grid_spec=pltpu.PrefetchScalarGridSpec(
            num_scalar_prefetch=6,
            grid=(n_blocks,),
            in_specs=[
                pl.BlockSpec((TM_MOE * SUBLANES, LANES), lambda b, be, nu, *_: (jnp.minimum(b, nu[0] - 1), 0)),
                any_spec,
                pl.BlockSpec((1, 1, 2 * DFF), lambda b, be, *_: (be[b], 0, 0)),
                any_spec,
                pl.BlockSpec((1, 1, D), lambda b, be, *_: (be[b], 0, 0)),
            ],
            out_specs=pl.BlockSpec((TM_MOE * SUBLANES, LANES), lambda b, *_: (b, 0)),
            scratch_shapes=[pltpu.VMEM((2, D, 2 * DFF), F32), pltpu.VMEM((2, DFF, D), F32),
                            pltpu.VMEM((D, 2 * DFF), BF16), pltpu.VMEM((DFF, D), BF16),
                            pltpu.SemaphoreType.DMA((2, 2))],
        ),
        out_shape=jax.ShapeDtypeStruct((n_slots * SUBLANES, LANES), F32),
        compiler_params=cparams(dimension_semantics=("arbitrary",)),
        name="moe_experts",
    )(block_e, n_used, block_ord.astype(I32), expert_order, n_distinct, block_valid, xs_sorted, w_up[0],
      b_up[0].reshape(NE, 1, 2 * DFF), w_down[0], b_down[0].reshape(NE, 1, D))

    n_tiles = T // TS_COMBINE
    npt = TP // TS_COMBINE
    y_p, y_s = pl.pallas_call(
        functools.partial(_combine_body, n_prompt_tiles=npt),
        grid=(n_tiles,),
        in_specs=[
            pl.BlockSpec((1, 1, TOP_K * TS_COMBINE), lambda i: (i, 0, 0), memory_space=pltpu.SMEM),
            pl.BlockSpec((1, 1, TOP_K * TS_COMBINE), lambda i: (jnp.minimum(i + 1, n_tiles - 1), 0, 0),
                         memory_space=pltpu.SMEM),
            pl.BlockSpec((1, 1, TOP_K * TS_COMBINE), lambda i: (i, 0, 0), memory_space=pltpu.SMEM),
            any_spec,
            pl.BlockSpec((TS_COMBINE * SUBLANES, LANES), lambda i: (i, 0)),
            pl.BlockSpec((TS_COMBINE, PLE), lambda i: (jnp.minimum(i, npt - 1), 0)),
            pl.BlockSpec((TS_COMBINE, PLE), lambda i: (jnp.maximum(i - npt, 0), 0)),
            _const_spec(wg_bf.shape), _const_spec(wp_bf.shape), _const_spec((1, D)),
        ],
        out_specs=[
            pl.BlockSpec((TS_COMBINE, D), lambda i: (jnp.minimum(i, npt - 1), 0)),
            pl.BlockSpec((TS_COMBINE, D), lambda i: (jnp.maximum(i - npt, 0), 0)),
        ],
        out_shape=[jax.ShapeDtypeStruct((TP, D), F32), jax.ShapeDtypeStruct((TSM, D), F32)],
        scratch_shapes=[pltpu.VMEM((2, TOP_K, TS_COMBINE * SUBLANES, LANES), F32),
                        pltpu.VMEM((TS_COMBINE * SUBLANES, LANES), F32),
                        pltpu.SemaphoreType.DMA((2,))],
        compiler_params=cparams(dimension_semantics=("arbitrary",)),
        name="moe_combine",
    )(per_tile(dest, TS_COMBINE), per_tile(dest, TS_COMBINE), per_tile(w_t, TS_COMBINE), ys_sorted,
      x1_all, p_prompt[0].reshape(TP, PLE), p_sample[0].transpose(1, 0, 2).reshape(TSM, PLE),
      wg_bf, wp_bf, nfin)

    y_prompt = y_p.reshape(B, L, D)
    y_sample = y_s.reshape(NL, NB, D).transpose(1, 0, 2)
    new_pool_prompt = pool_tail[:, CARRY_ROWS - POOL_BUF:, :][None]
    new_pool_sample = jnp.concatenate([state_pool[0][:, NL:, :], zp_s.transpose(1, 0, 2)], axis=1)[None]
    new_sgu_v_sample = vn_s.transpose(1, 0, 2)[None]
    return (y_prompt, y_sample, new_pool_prompt, new_pool_sample, new_sgu_v_sample)
```

```python
import functools

import jax
import jax.numpy as jnp
from jax import lax
from jax.experimental import pallas as pl
from jax.experimental.pallas import tpu as pltpu

F32 = jnp.float32
BF16 = jnp.bfloat16
I32 = jnp.int32

POOL_WINDOWS = (2, 4, 8, 16)
POOL_BUF = max(POOL_WINDOWS) - 1
SGU_HEADS = 4
CHUNK = 128
TOP_K = 4
SWIGLU_LIMIT = 7.0
SWIGLU_ALPHA = 1.702
RMS_EPS = 1e-6
LN_EPS = 1e-5

SUBLANES = 8
LANES = 128
CARRY_ROWS = 16
TS_MIX = 512
TS_SCATTER = 512
DMA_PRIORITIES = 2
ROW_DMA_UNROLL = 8
WEIGHT_SLOTS = 2
CAST_ROWS = 128
TM_MOE = 512
FF_CHUNK = 512
ROW_VARIANTS = 4
TS_COMBINE = 512
VMEM_LIMIT_BYTES = 56 * 1024 * 1024


def _rms(x, g):
    return x * lax.rsqrt(jnp.mean(x * x, axis=-1, keepdims=True) + RMS_EPS) * g


def _dot(a, b):
    return jnp.dot(a, b, preferred_element_type=F32)


def _load_rows(ref, n_rows):
    return jnp.concatenate([ref[pl.ds(c, n_rows, stride=SUBLANES), :] for c in range(SUBLANES)], axis=-1)


def _store_rows(ref, val):
    n_rows = val.shape[0]
    for c in range(SUBLANES):
        ref[pl.ds(c, n_rows, stride=SUBLANES), :] = val[:, c * LANES:(c + 1) * LANES]


def _gelu(x):
    return 0.5 * x * (1.0 + lax.erf(x * (2.0 ** -0.5)))


def _head_layernorm(v, g, b, hd):
    outs = []
    for h in range(SGU_HEADS):
        vh = v[:, h * hd:(h + 1) * hd]
        mu = jnp.mean(vh, axis=-1, keepdims=True)
        d = vh - mu
        var = jnp.mean(d * d, axis=-1, keepdims=True)
        outs.append(d * lax.rsqrt(var + LN_EPS) * g[:, h * hd:(h + 1) * hd] + b[:, h * hd:(h + 1) * hd])
    return outs


def _pool_project(diffs, wpool_ref, pscale):
    gd = diffs[0].shape[-1]
    outs = []
    for g, d in enumerate(diffs):
        outs.append(_dot(d.astype(BF16), wpool_ref[g]) * pscale[:, g * gd:(g + 1) * gd])
    return jnp.concatenate(outs, axis=-1).astype(BF16)


def _route_and_store(x, a_bf, b_bf, wout_ref, nf_ref, wrt_ref, br_ref, cnt_ref,
                     x1_ref, h2_ref, idx_ref, w_ref, rank_ref, cnt_out_ref):
    ts = x.shape[0]
    pw = a_bf.shape[-1]
    n_exp = wrt_ref.shape[0]
    mix = _dot(a_bf, wout_ref[:pw, :]) + _dot(b_bf, wout_ref[pw:, :])
    x1 = x + mix
    _store_rows(x1_ref, x1)
    h2 = _rms(x1, nf_ref[...])
    _store_rows(h2_ref, h2)
    logits = lax.dot_general(wrt_ref[...], h2, (((1,), (1,)), ((), ())),
                             precision=lax.Precision.HIGHEST,
                             preferred_element_type=F32) + br_ref[...]
    eio = lax.broadcasted_iota(I32, (n_exp, ts), 0).astype(F32)
    l = logits
    tops, sels = [], []
    for _ in range(TOP_K):
        m = jnp.max(l, axis=0, keepdims=True)
        sel = jnp.min(jnp.where(l == m, eio, float(n_exp)), axis=0, keepdims=True)
        tops.append(m)
        sels.append(sel)
        l = jnp.where(eio == sel, -jnp.inf, l)
    exps = [jnp.exp(t - tops[0]) for t in tops]
    denom = exps[0] + exps[1] + exps[2] + exps[3]
    onehot = jnp.zeros((n_exp, ts), F32)
    for sel in sels:
        onehot = onehot + jnp.where(eio == sel, 1.0, 0.0)
    r_i = lax.broadcasted_iota(I32, (ts, ts), 0)
    c_i = lax.broadcasted_iota(I32, (ts, ts), 1)
    upper = jnp.where(r_i < c_i, 1.0, 0.0).astype(BF16)
    base = cnt_ref[:, 0:1]
    rank_all = _dot(onehot.astype(BF16), upper) + base
    for k in range(TOP_K):
        idx_ref[k:k + 1, :] = sels[k].astype(I32)
        w_ref[k:k + 1, :] = exps[k] / denom
        rk = jnp.sum(jnp.where(eio == sels[k], rank_all, 0.0), axis=0, keepdims=True)
        rank_ref[k:k + 1, :] = rk.astype(I32)
    cnt_ref[...] = cnt_ref[...] + jnp.sum(onehot, axis=1, keepdims=True)
    cnt_out_ref[...] = cnt_ref[...]


def _prompt_tile(t, x_ref, nm_ref, win_ref, wpool_ref, pscale_ref, lng_ref, lnb_ref, sw_ref, sbt_ref,
                 tail_ref, carry_ref, bo_ref, n_tiles, route):
    ts = x_ref.shape[0]
    pw = pscale_ref.shape[-1]
    gd = pw // len(POOL_WINDOWS)
    hd = pw // SGU_HEADS

    @pl.when(t == 0)
    def _():
        carry_ref[...] = jnp.zeros_like(carry_ref)

    x = x_ref[...]
    h = _rms(x, nm_ref[...])
    z = _dot(h.astype(BF16), win_ref[...])
    zp = z[:, :pw]

    ext = jnp.concatenate([carry_ref[...], zp], axis=0)
    carry_ref[...] = zp[ts - CARRY_ROWS:, :]

    @pl.when(t == n_tiles - 1)
    def _():
        tail_ref[0] = zp[ts - CARRY_ROWS:, :]

    pos = t * ts + lax.broadcasted_iota(I32, (ts, 1), 0)
    diffs = []
    for g, w in enumerate(POOL_WINDOWS):
        acc = ext[:, g * gd:(g + 1) * gd]
        k = 1
        while k < w:
            acc = acc + pltpu.roll(acc, k, 0)
            k *= 2
        cnt = jnp.minimum(w, pos + 1).astype(F32)
        diffs.append(acc[CARRY_ROWS:, :] / cnt - zp[:, g * gd:(g + 1) * gd])
    a_bf = _pool_project(diffs, wpool_ref, pscale_ref[...])

    zuv = _gelu(z[:, pw:])
    u = zuv[:, :pw]
    vn = _head_layernorm(zuv[:, pw:], lng_ref[...], lnb_ref[...], hd)
    ri = lax.broadcasted_iota(I32, (CHUNK, CHUNK), 0)
    ci = lax.broadcasted_iota(I32, (CHUNK, CHUNK), 1)
    for hh in range(SGU_HEADS):
        wm = jnp.where(ci <= ri, sw_ref[hh], 0.0).astype(BF16)
        bias = sbt_ref[:, hh:hh + 1]
        vh = vn[hh].astype(BF16)
        for c in range(ts // CHUNK):
            rows = slice(c * CHUNK, (c + 1) * CHUNK)
            mixed = _dot(wm, vh[rows, :]) + bias
            bo_ref[rows, hh * hd:(hh + 1) * hd] = (u[rows, hh * hd:(hh + 1) * hd] * mixed).astype(BF16)
    route(x, a_bf, bo_ref[...])


def _sample_tile(x_ref, st_ref, nm_ref, win_ref, wpool_ref, pscale_ref, lng_ref, lnb_ref, sw_ref,
                 sb_ref, zp_ref, vn_ref, route):
    n_new, nb, pw = zp_ref.shape
    gd = pw // len(POOL_WINDOWS)
    hd = pw // SGU_HEADS

    x = x_ref[...]
    h = _rms(x, nm_ref[...])
    z = _dot(h.astype(BF16), win_ref[...])
    zp = z[:, :pw]
    for l in range(n_new):
        zp_ref[l] = zp[l * nb:(l + 1) * nb, :]

    diffs = []
    for g, w in enumerate(POOL_WINDOWS):
        cols = slice(g * gd, (g + 1) * gd)
        lo_needed = POOL_BUF + 1 - w
        suffix = {}
        run = None
        for r in range(POOL_BUF - 1, lo_needed - 1, -1):
            slab = st_ref[r][:, cols]
            run = slab if run is None else run + slab
            suffix[r] = run
        pieces = []
        for l in range(n_new):
            first_hist = POOL_BUF + 1 + l - w
            tot = suffix[first_hist] if first_hist < POOL_BUF else None
            for l2 in range(max(0, l - w + 1), l + 1):
                zl = zp[l2 * nb:(l2 + 1) * nb, cols]
                tot = zl if tot is None else tot + zl
            pieces.append(tot / float(w) - zp[l * nb:(l + 1) * nb, cols])
        diffs.append(jnp.concatenate(pieces, axis=0))
    a_bf = _pool_project(diffs, wpool_ref, pscale_ref[...])

    zuv = _gelu(z[:, pw:])
    u = zuv[:, :pw]
    vn = _head_layernorm(zuv[:, pw:], lng_ref[...], lnb_ref[...], hd)
    vfull = jnp.concatenate(vn, axis=-1)
    for l in range(n_new):
        vn_ref[l] = vfull[l * nb:(l + 1) * nb, :]
    b_cols = []
    for hh in range(SGU_HEADS):
        pieces = []
        for l in range(n_new):
            mixed = None
            for l2 in range(l + 1):
                term = sw_ref[(hh * n_new + l) * n_new + l2] * vn[hh][l2 * nb:(l2 + 1) * nb, :]
                mixed = term if mixed is None else mixed + term
            mixed = mixed + sb_ref[hh * n_new + l]
            pieces.append(u[l * nb:(l + 1) * nb, hh * hd:(hh + 1) * hd] * mixed)
        b_cols.append(jnp.concatenate(pieces, axis=0))
    route(x, a_bf, jnp.concatenate(b_cols, axis=-1).astype(BF16))


def _mixer_body(xp_ref, xs_ref, st_ref, nm_ref, win_ref, wpool_ref, pscale_ref, lng_ref, lnb_ref,
                sw_ref, sbt_ref, sws_ref, sbs_ref, wout_ref, nf_ref, wrt_ref, br_ref,
                x1_ref, h2_ref, idx_ref, w_ref, rank_ref, cnt_out_ref, tail_ref, zp_ref, vn_ref,
                carry_ref, cnt_ref, bo_ref, *, tiles_per_prompt):
    i = pl.program_id(0)
    n_prompt_steps = pl.num_programs(0) - 1

    @pl.when(i == 0)
    def _():
        cnt_ref[...] = jnp.zeros_like(cnt_ref)

    route = functools.partial(_route_and_store, wout_ref=wout_ref, nf_ref=nf_ref, wrt_ref=wrt_ref,
                              br_ref=br_ref, cnt_ref=cnt_ref, x1_ref=x1_ref, h2_ref=h2_ref,
                              idx_ref=idx_ref,
                              w_ref=w_ref, rank_ref=rank_ref, cnt_out_ref=cnt_out_ref)

    @pl.when(i < n_prompt_steps)
    def _():
        _prompt_tile(lax.rem(i, tiles_per_prompt), xp_ref, nm_ref, win_ref, wpool_ref, pscale_ref,
                     lng_ref, lnb_ref, sw_ref, sbt_ref, tail_ref, carry_ref, bo_ref,
                     tiles_per_prompt, route)

    @pl.when(i == n_prompt_steps)
    def _():
        _sample_tile(xs_ref, st_ref, nm_ref, win_ref, wpool_ref, pscale_ref, lng_ref, lnb_ref,
                     sws_ref, sbs_ref, zp_ref, vn_ref, route)


def _row_copy(src, src_row, dst, dst_row, sem):
    s_rows = pl.ds(pl.multiple_of(src_row * SUBLANES, SUBLANES), SUBLANES)
    d_rows = pl.ds(pl.multiple_of(dst_row * SUBLANES, SUBLANES), SUBLANES)
    return pltpu.make_async_copy(src.at[s_rows], dst.at[d_rows], sem)


def _scatter_body(pstart_ref, pcnt_ref, nused_ref, dest_ref, x1_ref, xs_hbm, zero_ref, zsem, rsem):
    i = pl.program_id(0)
    ts = x1_ref.shape[0] // SUBLANES
    tm = zero_ref.shape[0] // SUBLANES
    n_exp = pstart_ref.shape[0]

    def zero_block(row):
        rows = pl.ds(pl.multiple_of(row * SUBLANES, SUBLANES), tm * SUBLANES)
        return pltpu.make_async_copy(zero_ref, xs_hbm.at[rows], zsem)

    @pl.when(i == 0)
    def _():
        zero_ref[...] = jnp.zeros_like(zero_ref)
        for e in range(n_exp):
            zero_block(pstart_ref[e] + pcnt_ref[e]).start()
        for e in range(n_exp):
            zero_block(pstart_ref[e] + pcnt_ref[e]).wait()

        n_total = xs_hbm.shape[0] // (tm * SUBLANES)

        def start_tail(b, c):
            zero_block(b * tm).start()
            return c

        def wait_tail(b, c):
            zero_block(b * tm).wait()
            return c

        lax.fori_loop(nused_ref[0], n_total, start_tail, 0)
        lax.fori_loop(nused_ref[0], n_total, wait_tail, 0)

    def start_rows(r, carry):
        for k in range(TOP_K):
            _row_copy(x1_ref, r, xs_hbm, dest_ref[0, 0, k * ts + r], rsem).start(priority=k % DMA_PRIORITIES)
        return carry

    lax.fori_loop(0, ts, start_rows, 0, unroll=ROW_DMA_UNROLL)

    def wait_rows(r, carry):
        for k in range(TOP_K):
            _row_copy(x1_ref, 0, xs_hbm, 0, rsem).wait()
        return carry

    lax.fori_loop(0, ts, wait_rows, 0, unroll=ROW_DMA_UNROLL)


def _experts_body(be_ref, nused_ref, eord_ref, eorder_ref, ndist_ref, bvalid_ref,
                  xs_ref, wup_hbm, bup_ref, wdn_hbm, bdn_ref, ys_ref,
                  wup_f32, wdn_f32, wup_bf, wdn_bf, wsem):
    b = pl.program_id(0)
    d_ff = wdn_bf.shape[0]

    def weight_copies(e, slot):
        return (pltpu.make_async_copy(wup_hbm.at[e], wup_f32.at[slot], wsem.at[slot, 0]),
                pltpu.make_async_copy(wdn_hbm.at[e], wdn_f32.at[slot], wsem.at[slot, 1]))

    @pl.when(b == 0)
    def _():
        for cp in weight_copies(eorder_ref[0], 0):
            cp.start()

    @pl.when(b < nused_ref[0])
    def _():
        prev = be_ref[jnp.maximum(b - 1, 0)]

        @pl.when((b == 0) | (be_ref[b] != prev))
        def _():
            j = eord_ref[b]
            slot = lax.rem(j, WEIGHT_SLOTS)
            for cp in weight_copies(be_ref[b], slot):
                cp.wait()

            @pl.when(j + 1 < ndist_ref[0])
            def _():
                for cp in weight_copies(eorder_ref[j + 1], lax.rem(j + 1, WEIGHT_SLOTS)):
                    cp.start()

            rows = CAST_ROWS

            def cast_up(r, c):
                sl = pl.ds(pl.multiple_of(r * rows, rows), rows)
                wup_bf[sl, :] = wup_f32[slot, sl, :].astype(BF16)
                return c

            lax.fori_loop(0, wup_bf.shape[0] // rows, cast_up, 0)

            def cast_dn(r, c):
                sl = pl.ds(pl.multiple_of(r * rows, rows), rows)
                wdn_bf[sl, :] = wdn_f32[slot, sl, :].astype(BF16)
                return c

            lax.fori_loop(0, wdn_bf.shape[0] // rows, cast_dn, 0)

        def compute(n_rows):
            xb = _load_rows(xs_ref, n_rows).astype(BF16)
            acc = None
            for j in range(0, d_ff, FF_CHUNK):
                glu = _dot(xb, wup_bf[:, j:j + FF_CHUNK]) + bup_ref[0, :, j:j + FF_CHUNK]
                lin = (_dot(xb, wup_bf[:, d_ff + j:d_ff + j + FF_CHUNK])
                       + bup_ref[0, :, d_ff + j:d_ff + j + FF_CHUNK])
                glu = jnp.minimum(glu, SWIGLU_LIMIT)
                lin = jnp.clip(lin, -SWIGLU_LIMIT, SWIGLU_LIMIT)
                act = glu * jax.nn.sigmoid(SWIGLU_ALPHA * glu) * (lin + 1.0)
                part = _dot(act.astype(BF16), wdn_bf[j:j + FF_CHUNK, :])
                acc = part if acc is None else acc + part
            _store_rows(ys_ref, acc + bdn_ref[0])
            if n_rows * SUBLANES < ys_ref.shape[0]:
                ys_ref[n_rows * SUBLANES:, :] = jnp.zeros(
                    (ys_ref.shape[0] - n_rows * SUBLANES, ys_ref.shape[1]), ys_ref.dtype)

        tm = ys_ref.shape[0] // SUBLANES
        step = tm // ROW_VARIANTS
        valid = bvalid_ref[b]
        for hi in range(step, tm + 1, step):
            lo = hi - step if hi > step else -1
            @pl.when((valid > lo) & (valid <= hi))
            def _(hi=hi):
                compute(hi)

    @pl.when(b >= nused_ref[0])
    def _():
        ys_ref[...] = jnp.zeros_like(ys_ref)


def _combine_body(dcur_ref, dnext_ref, wsm_ref, ys_hbm, x1_ref, pp_ref, ps_ref, wg_ref, wp_ref, nfin_ref,
                  yp_ref, ysm_ref, buf, acc_ref, sem, *, n_prompt_tiles):
    i = pl.program_id(0)
    n = pl.num_programs(0)
    ts = x1_ref.shape[0] // SUBLANES
    slot = i % 2

    def issue_row(dref, sl, r):
        for k in range(TOP_K):
            cp = _row_copy(ys_hbm, dref[0, 0, k * ts + r], buf.at[sl, k], r, sem.at[sl])
            cp.start(priority=k % DMA_PRIORITIES)

    @pl.when(i == 0)
    def _():
        def body(r, c):
            issue_row(dcur_ref, 0, r)
            return c
        lax.fori_loop(0, ts, body, 0, unroll=ROW_DMA_UNROLL)

    def wait_rows(r, c):
        for k in range(TOP_K):
            _row_copy(ys_hbm, 0, buf.at[slot, k], 0, sem.at[slot]).wait()
        return c

    lax.fori_loop(0, ts, wait_rows, 0, unroll=ROW_DMA_UNROLL)

    def sum_rows(start_next):
        def body(r, c):
            rows = pl.ds(pl.multiple_of(r * SUBLANES, SUBLANES), SUBLANES)
            acc = x1_ref[rows, :]
            for k in range(TOP_K):
                acc = acc + buf[slot, k, rows, :] * wsm_ref[0, 0, k * ts + r]
            acc_ref[rows, :] = acc
            if start_next:
                issue_row(dnext_ref, 1 - slot, r)
            return c
        lax.fori_loop(0, ts, body, 0, unroll=ROW_DMA_UNROLL)

    @pl.when(i + 1 < n)
    def _():
        sum_rows(True)

    @pl.when(i + 1 == n)
    def _():
        sum_rows(False)

    acc = _load_rows(acc_ref, ts)
    gate = jax.nn.sigmoid(_dot(acc.astype(BF16), wg_ref[...]))
    p = jnp.where(i < n_prompt_tiles, pp_ref[...], ps_ref[...])
    proj = _dot(p.astype(BF16), wp_ref[...])
    y = _rms(acc + gate * proj, nfin_ref[...])

    @pl.when(i < n_prompt_tiles)
    def _():
        yp_ref[...] = y

    @pl.when(i >= n_prompt_tiles)
    def _():
        ysm_ref[...] = y


def _const_spec(shape):
    nd = len(shape)
    return pl.BlockSpec(shape, lambda *_: (0,) * nd)


def kernel(x_prompt, x_sample, state_pool, p_prompt, p_sample, norm_mix, w_in, w_pool, pool_scale,
           sgu_ln_g, sgu_ln_b, sgu_w, sgu_b, w_out, norm_ffn, w_router, b_router, w_up, b_up,
           w_down, b_down, w_ple_gate, w_ple_proj, norm_final):
    B, L, D = x_prompt.shape
    NB, NL, _ = x_sample.shape
    PW = state_pool.shape[-1]
    NE = w_router.shape[-1]
    DFF = w_down.shape[2]
    PLE = p_prompt.shape[-1]
    TP = B * L
    TSM = NB * NL
    T = TP + TSM
    assert w_in.shape[0] == 1, "single trunk layer"
    assert L % TS_MIX == 0 and TSM == TS_MIX and TS_MIX % CHUNK == 0
    assert T % TS_SCATTER == 0 and TP % TS_COMBINE == 0 and TSM % TS_COMBINE == 0
    assert D == SUBLANES * LANES, "a token row must fill exactly one vreg tile"
    assert state_pool.shape[2] == POOL_BUF and PW % len(POOL_WINDOWS) == 0 and PW % SGU_HEADS == 0

    nm = norm_mix[0].reshape(1, D)
    nf = norm_ffn[0].reshape(1, D)
    nfin = norm_final.reshape(1, D)
    win_bf = w_in[0].astype(BF16)
    wpool_bf = w_pool[0].astype(BF16)
    pscale = pool_scale[0].reshape(1, PW)
    lng = sgu_ln_g[0].reshape(1, PW)
    lnb = sgu_ln_b[0].reshape(1, PW)
    wout_bf = w_out[0].astype(BF16)
    wrt = w_router[0].T
    br = b_router[0].reshape(NE, 1)
    wg_bf = w_ple_gate[0].astype(BF16)
    wp_bf = w_ple_proj[0].astype(BF16)

    cparams = functools.partial(pltpu.CompilerParams, vmem_limit_bytes=VMEM_LIMIT_BYTES)
    NT = L // TS_MIX
    n_prompt_steps = B * NT
    xs_rows = x_sample.transpose(1, 0, 2).reshape(TSM, D)
    hist = state_pool[0].transpose(1, 0, 2)
    sw_small = sgu_w[0, :, :NL, :NL].reshape(-1)
    sb_small = sgu_b[0, :, :NL].reshape(-1)
    smem = pl.BlockSpec(memory_space=pltpu.SMEM)
    any_spec = pl.BlockSpec(memory_space=pl.ANY)
    last_prompt = n_prompt_steps - 1
    x1_all, h2_all, idx_t, w_t, rank_t, counts, pool_tail, zp_s, vn_s = pl.pallas_call(
        functools.partial(_mixer_body, tiles_per_prompt=NT),
        grid=(n_prompt_steps + 1,),
        in_specs=[
            pl.BlockSpec((TS_MIX, D), lambda i: (jnp.minimum(i, last_prompt), 0)),
            _const_spec((TSM, D)), _const_spec(hist.shape),
            _const_spec((1, D)), _const_spec(win_bf.shape), _const_spec(wpool_bf.shape),
            _const_spec((1, PW)), _const_spec((1, PW)), _const_spec((1, PW)),
            _const_spec(sgu_w[0].shape), _const_spec((CHUNK, SGU_HEADS)), smem, smem,
            _const_spec(wout_bf.shape), _const_spec((1, D)), _const_spec(wrt.shape),
            _const_spec((NE, 1)),
        ],
        out_specs=[
            pl.BlockSpec((TS_MIX * SUBLANES, LANES), lambda i: (i, 0)),
            pl.BlockSpec((TS_MIX * SUBLANES, LANES), lambda i: (i, 0)),
            pl.BlockSpec((TOP_K, TS_MIX), lambda i: (0, i)),
            pl.BlockSpec((TOP_K, TS_MIX), lambda i: (0, i)),
            pl.BlockSpec((TOP_K, TS_MIX), lambda i: (0, i)),
            _const_spec((NE, LANES)),
            pl.BlockSpec((1, CARRY_ROWS, PW), lambda i: (jnp.minimum(i, last_prompt) // NT, 0, 0)),
            _const_spec((NL, NB, PW)),
            _const_spec((NL, NB, PW)),
        ],
        out_shape=[
            jax.ShapeDtypeStruct((T * SUBLANES, LANES), F32),
            jax.ShapeDtypeStruct((T * SUBLANES, LANES), F32),
            jax.ShapeDtypeStruct((TOP_K, T), I32),
            jax.ShapeDtypeStruct((TOP_K, T), F32),
            jax.ShapeDtypeStruct((TOP_K, T), I32),
            jax.ShapeDtypeStruct((NE, LANES), F32),
            jax.ShapeDtypeStruct((B, CARRY_ROWS, PW), F32),
            jax.ShapeDtypeStruct((NL, NB, PW), F32),
            jax.ShapeDtypeStruct((NL, NB, PW), F32),
        ],
        scratch_shapes=[pltpu.VMEM((CARRY_ROWS, PW), F32), pltpu.VMEM((NE, LANES), F32),
                        pltpu.VMEM((TS_MIX, PW), BF16)],
        compiler_params=cparams(dimension_semantics=("arbitrary",)),
        name="mixer",
    )(x_prompt.reshape(TP, D), xs_rows, hist, nm, win_bf, wpool_bf, pscale, lng, lnb, sgu_w[0],
      sgu_b[0].T, sw_small, sb_small, wout_bf, nf, wrt, br)

    A = T * TOP_K
    n_blocks = -(-A // TM_MOE) + NE
    n_slots = n_blocks * TM_MOE
    cnt_i = counts[:, 0].astype(I32)
    padded = (cnt_i + TM_MOE - 1) // TM_MOE * TM_MOE
    pad_end = jnp.cumsum(padded)
    pad_start = pad_end - padded
    n_used = (pad_end[-1] // TM_MOE).astype(I32).reshape(1)
    blk = jnp.minimum(jnp.arange(n_blocks, dtype=I32), n_used[0] - 1)
    block_e = jnp.minimum(jnp.sum(pad_end[None, :] <= (blk * TM_MOE)[:, None], axis=1), NE - 1).astype(I32)
    expert_ids = jnp.arange(NE, dtype=I32)[:, None, None]
    dest = jnp.sum(jnp.where(idx_t[None] == expert_ids, pad_start[:, None, None], 0), axis=0) + rank_t
    used = padded > 0
    ordinal = jnp.cumsum(used.astype(I32)) - 1
    n_distinct = jnp.sum(used.astype(I32)).reshape(1)
    block_ord = jnp.sum(jnp.where(block_e[:, None] == expert_ids[:, 0, 0][None, :], ordinal[None, :], 0), axis=1)
    slots_j = jnp.minimum(jnp.arange(NE + 1, dtype=I32), n_distinct[0] - 1)
    expert_order = jnp.sum(jnp.where(used[None, :] & (ordinal[None, :] == slots_j[:, None]),
                                     expert_ids[:, 0, 0][None, :], 0), axis=1).astype(I32)

    onehot_be = block_e[:, None] == expert_ids[:, 0, 0][None, :]
    first_row = jnp.sum(jnp.where(onehot_be, pad_start[None, :], 0), axis=1)
    rows_left = jnp.sum(jnp.where(onehot_be, cnt_i[None, :], 0), axis=1) - (blk * TM_MOE - first_row)
    block_valid = jnp.where(jnp.arange(n_blocks, dtype=I32) < n_used[0],
                            jnp.clip(rows_left, 0, TM_MOE), 0).astype(I32)

    def per_tile(a, ts):
        return a.reshape(TOP_K, T // ts, ts).transpose(1, 0, 2).reshape(T // ts, 1, TOP_K * ts)

    xs_sorted = pl.pallas_call(
        _scatter_body,
        grid_spec=pltpu.PrefetchScalarGridSpec(
            num_scalar_prefetch=3,
            grid=(T // TS_SCATTER,),
            in_specs=[pl.BlockSpec((1, 1, TOP_K * TS_SCATTER), lambda i, *_: (i, 0, 0),
                                   memory_space=pltpu.SMEM),
                      pl.BlockSpec((TS_SCATTER * SUBLANES, LANES), lambda i, *_: (i, 0))],
            out_specs=any_spec,
            scratch_shapes=[pltpu.VMEM((TM_MOE * SUBLANES, LANES), F32), pltpu.SemaphoreType.DMA(()),
                            pltpu.SemaphoreType.DMA(())],
        ),
        out_shape=jax.ShapeDtypeStruct(((n_slots + TM_MOE) * SUBLANES, LANES), F32),
        compiler_params=cparams(dimension_semantics=("arbitrary",)),
        name="moe_scatter",
    )(pad_start, cnt_i, n_used, per_tile(dest, TS_SCATTER), h2_all)

    ys_sorted = pl.pallas_call(
        _experts_body,
        grid_spec=pltpu.PrefetchScalarGridSpec(
            num_scalar_prefetch=6,
            grid=(n_blocks,),
            in_specs=[
                pl.BlockSpec((TM_MOE * SUBLANES, LANES), lambda b, be, nu, *_: (jnp.minimum(b, nu[0] - 1), 0)),
                any_spec,
                pl.BlockSpec((1, 1, 2 * DFF), lambda b, be, *_: (be[b], 0, 0)),
                any_spec,
                pl.BlockSpec((1, 1, D), lambda b, be, *_: (be[b], 0, 0)),
            ],
            out_specs=pl.BlockSpec((TM_MOE * SUBLANES, LANES), lambda b, *_: (b, 0)),
            scratch_shapes=[pltpu.VMEM((WEIGHT_SLOTS, D, 2 * DFF), F32), pltpu.VMEM((WEIGHT_SLOTS, DFF, D), F32),
                            pltpu.VMEM((D, 2 * DFF), BF16), pltpu.VMEM((DFF, D), BF16),
                            pltpu.SemaphoreType.DMA((WEIGHT_SLOTS, 2))],
        ),
        out_shape=jax.ShapeDtypeStruct((n_slots * SUBLANES, LANES), F32),
        compiler_params=cparams(dimension_semantics=("arbitrary",)),
        name="moe_experts",
    )(block_e, n_used, block_ord.astype(I32), expert_order, n_distinct, block_valid, xs_sorted, w_up[0],
      b_up[0].reshape(NE, 1, 2 * DFF), w_down[0], b_down[0].reshape(NE, 1, D))

    n_tiles = T // TS_COMBINE
    npt = TP // TS_COMBINE
    y_p, y_s = pl.pallas_call(
        functools.partial(_combine_body, n_prompt_tiles=npt),
        grid=(n_tiles,),
        in_specs=[
            pl.BlockSpec((1, 1, TOP_K * TS_COMBINE), lambda i: (i, 0, 0), memory_space=pltpu.SMEM),
            pl.BlockSpec((1, 1, TOP_K * TS_COMBINE), lambda i: (jnp.minimum(i + 1, n_tiles - 1), 0, 0),
                         memory_space=pltpu.SMEM),
            pl.BlockSpec((1, 1, TOP_K * TS_COMBINE), lambda i: (i, 0, 0), memory_space=pltpu.SMEM),
            any_spec,
            pl.BlockSpec((TS_COMBINE * SUBLANES, LANES), lambda i: (i, 0)),
            pl.BlockSpec((TS_COMBINE, PLE), lambda i: (jnp.minimum(i, npt - 1), 0)),
            pl.BlockSpec((TS_COMBINE, PLE), lambda i: (jnp.maximum(i - npt, 0), 0)),
            _const_spec(wg_bf.shape), _const_spec(wp_bf.shape), _const_spec((1, D)),
        ],
        out_specs=[
            pl.BlockSpec((TS_COMBINE, D), lambda i: (jnp.minimum(i, npt - 1), 0)),
            pl.BlockSpec((TS_COMBINE, D), lambda i: (jnp.maximum(i - npt, 0), 0)),
        ],
        out_shape=[jax.ShapeDtypeStruct((TP, D), F32), jax.ShapeDtypeStruct((TSM, D), F32)],
        scratch_shapes=[pltpu.VMEM((2, TOP_K, TS_COMBINE * SUBLANES, LANES), F32),
                        pltpu.VMEM((TS_COMBINE * SUBLANES, LANES), F32),
                        pltpu.SemaphoreType.DMA((2,))],
        compiler_params=cparams(dimension_semantics=("arbitrary",)),
        name="moe_combine",
    )(per_tile(dest, TS_COMBINE), per_tile(dest, TS_COMBINE), per_tile(w_t, TS_COMBINE), ys_sorted,
      x1_all, p_prompt[0].reshape(TP, PLE), p_sample[0].transpose(1, 0, 2).reshape(TSM, PLE),
      wg_bf, wp_bf, nfin)

    y_prompt = y_p.reshape(B, L, D)
    y_sample = y_s.reshape(NL, NB, D).transpose(1, 0, 2)
    new_pool_prompt = pool_tail[:, CARRY_ROWS - POOL_BUF:, :][None]
    new_pool_sample = jnp.concatenate([state_pool[0][:, NL:, :], zp_s.transpose(1, 0, 2)], axis=1)[None]
    new_sgu_v_sample = vn_s.transpose(1, 0, 2)[None]
    return (y_prompt, y_sample, new_pool_prompt, new_pool_sample, new_sgu_v_sample)
```

```python
import functools

import jax
import jax.numpy as jnp
from jax import lax
from jax.experimental import pallas as pl
from jax.experimental.pallas import tpu as pltpu

F32 = jnp.float32
BF16 = jnp.bfloat16
I32 = jnp.int32

POOL_WINDOWS = (2, 4, 8, 16)
POOL_BUF = max(POOL_WINDOWS) - 1
SGU_HEADS = 4
CHUNK = 128
TOP_K = 4
SWIGLU_LIMIT = 7.0
SWIGLU_ALPHA = 1.702
RMS_EPS = 1e-6
LN_EPS = 1e-5

SUBLANES = 8
LANES = 128
CARRY_ROWS = 16
TS_MIX = 512
TS_SCATTER = 512
DMA_PRIORITIES = 2
ROW_DMA_UNROLL = 8
WEIGHT_SLOTS = 2
TM_MOE = 512
FF_CHUNK = 512
ROW_VARIANTS = 4
TS_COMBINE = 512
VMEM_LIMIT_BYTES = 56 * 1024 * 1024


def _rms(x, g):
    return x * lax.rsqrt(jnp.mean(x * x, axis=-1, keepdims=True) + RMS_EPS) * g


def _dot(a, b):
    return jnp.dot(a, b, preferred_element_type=F32)


def _load_rows(ref, n_rows):
    return jnp.concatenate([ref[pl.ds(c, n_rows, stride=SUBLANES), :] for c in range(SUBLANES)], axis=-1)


def _store_rows(ref, val):
    n_rows = val.shape[0]
    for c in range(SUBLANES):
        ref[pl.ds(c, n_rows, stride=SUBLANES), :] = val[:, c * LANES:(c + 1) * LANES]


def _gelu(x):
    return 0.5 * x * (1.0 + lax.erf(x * (2.0 ** -0.5)))


def _head_layernorm(v, g, b, hd):
    outs = []
    for h in range(SGU_HEADS):
        vh = v[:, h * hd:(h + 1) * hd]
        mu = jnp.mean(vh, axis=-1, keepdims=True)
        d = vh - mu
        var = jnp.mean(d * d, axis=-1, keepdims=True)
        outs.append(d * lax.rsqrt(var + LN_EPS) * g[:, h * hd:(h + 1) * hd] + b[:, h * hd:(h + 1) * hd])
    return outs


def _pool_project(diffs, wpool_ref, pscale):
    gd = diffs[0].shape[-1]
    outs = []
    for g, d in enumerate(diffs):
        outs.append(_dot(d.astype(BF16), wpool_ref[g]) * pscale[:, g * gd:(g + 1) * gd])
    return jnp.concatenate(outs, axis=-1).astype(BF16)


def _route_and_store(x, a_bf, b_bf, wout_ref, nf_ref, wrt_ref, br_ref, cnt_ref,
                     x1_ref, h2_ref, idx_ref, w_ref, rank_ref, cnt_out_ref):
    ts = x.shape[0]
    pw = a_bf.shape[-1]
    n_exp = wrt_ref.shape[0]
    mix = _dot(a_bf, wout_ref[:pw, :]) + _dot(b_bf, wout_ref[pw:, :])
    x1 = x + mix
    _store_rows(x1_ref, x1)
    h2 = _rms(x1, nf_ref[...])
    _store_rows(h2_ref, h2)
    logits = lax.dot_general(wrt_ref[...], h2, (((1,), (1,)), ((), ())),
                             precision=lax.Precision.HIGHEST,
                             preferred_element_type=F32) + br_ref[...]
    eio = lax.broadcasted_iota(I32, (n_exp, ts), 0).astype(F32)
    l = logits
    tops, sels = [], []
    for _ in range(TOP_K):
        m = jnp.max(l, axis=0, keepdims=True)
        sel = jnp.min(jnp.where(l == m, eio, float(n_exp)), axis=0, keepdims=True)
        tops.append(m)
        sels.append(sel)
        l = jnp.where(eio == sel, -jnp.inf, l)
    exps = [jnp.exp(t - tops[0]) for t in tops]
    denom = exps[0] + exps[1] + exps[2] + exps[3]
    onehot = jnp.zeros((n_exp, ts), F32)
    for sel in sels:
        onehot = onehot + jnp.where(eio == sel, 1.0, 0.0)
    r_i = lax.broadcasted_iota(I32, (ts, ts), 0)
    c_i = lax.broadcasted_iota(I32, (ts, ts), 1)
    upper = jnp.where(r_i < c_i, 1.0, 0.0).astype(BF16)
    base = cnt_ref[:, 0:1]
    rank_all = _dot(onehot.astype(BF16), upper) + base
    for k in range(TOP_K):
        idx_ref[k:k + 1, :] = sels[k].astype(I32)
        w_ref[k:k + 1, :] = exps[k] / denom
        rk = jnp.sum(jnp.where(eio == sels[k], rank_all, 0.0), axis=0, keepdims=True)
        rank_ref[k:k + 1, :] = rk.astype(I32)
    cnt_ref[...] = cnt_ref[...] + jnp.sum(onehot, axis=1, keepdims=True)
    cnt_out_ref[...] = cnt_ref[...]


def _prompt_tile(t, x_ref, nm_ref, win_ref, wpool_ref, pscale_ref, lng_ref, lnb_ref, sw_ref, sbt_ref,
                 tail_ref, carry_ref, bo_ref, n_tiles, route):
    ts = x_ref.shape[0]
    pw = pscale_ref.shape[-1]
    gd = pw // len(POOL_WINDOWS)
    hd = pw // SGU_HEADS

    @pl.when(t == 0)
    def _():
        carry_ref[...] = jnp.zeros_like(carry_ref)

    x = x_ref[...]
    h = _rms(x, nm_ref[...])
    z = _dot(h.astype(BF16), win_ref[...])
    zp = z[:, :pw]

    ext = jnp.concatenate([carry_ref[...], zp], axis=0)
    carry_ref[...] = zp[ts - CARRY_ROWS:, :]

    @pl.when(t == n_tiles - 1)
    def _():
        tail_ref[0] = zp[ts - CARRY_ROWS:, :]

    pos = t * ts + lax.broadcasted_iota(I32, (ts, 1), 0)
    diffs = []
    for g, w in enumerate(POOL_WINDOWS):
        acc = ext[:, g * gd:(g + 1) * gd]
        k = 1
        while k < w:
            acc = acc + pltpu.roll(acc, k, 0)
            k *= 2
        cnt = jnp.minimum(w, pos + 1).astype(F32)
        diffs.append(acc[CARRY_ROWS:, :] / cnt - zp[:, g * gd:(g + 1) * gd])
    a_bf = _pool_project(diffs, wpool_ref, pscale_ref[...])

    zuv = _gelu(z[:, pw:])
    u = zuv[:, :pw]
    vn = _head_layernorm(zuv[:, pw:], lng_ref[...], lnb_ref[...], hd)
    ri = lax.broadcasted_iota(I32, (CHUNK, CHUNK), 0)
    ci = lax.broadcasted_iota(I32, (CHUNK, CHUNK), 1)
    for hh in range(SGU_HEADS):
        wm = jnp.where(ci <= ri, sw_ref[hh], 0.0).astype(BF16)
        bias = sbt_ref[:, hh:hh + 1]
        vh = vn[hh].astype(BF16)
        for c in range(ts // CHUNK):
            rows = slice(c * CHUNK, (c + 1) * CHUNK)
            mixed = _dot(wm, vh[rows, :]) + bias
            bo_ref[rows, hh * hd:(hh + 1) * hd] = (u[rows, hh * hd:(hh + 1) * hd] * mixed).astype(BF16)
    route(x, a_bf, bo_ref[...])


def _sample_tile(x_ref, st_ref, nm_ref, win_ref, wpool_ref, pscale_ref, lng_ref, lnb_ref, sw_ref,
                 sb_ref, zp_ref, vn_ref, route):
    n_new, nb, pw = zp_ref.shape
    gd = pw // len(POOL_WINDOWS)
    hd = pw // SGU_HEADS

    x = x_ref[...]
    h = _rms(x, nm_ref[...])
    z = _dot(h.astype(BF16), win_ref[...])
    zp = z[:, :pw]
    for l in range(n_new):
        zp_ref[l] = zp[l * nb:(l + 1) * nb, :]

    diffs = []
    for g, w in enumerate(POOL_WINDOWS):
        cols = slice(g * gd, (g + 1) * gd)
        lo_needed = POOL_BUF + 1 - w
        suffix = {}
        run = None
        for r in range(POOL_BUF - 1, lo_needed - 1, -1):
            slab = st_ref[r][:, cols]
            run = slab if run is None else run + slab
            suffix[r] = run
        pieces = []
        for l in range(n_new):
            first_hist = POOL_BUF + 1 + l - w
            tot = suffix[first_hist] if first_hist < POOL_BUF else None
            for l2 in range(max(0, l - w + 1), l + 1):
                zl = zp[l2 * nb:(l2 + 1) * nb, cols]
                tot = zl if tot is None else tot + zl
            pieces.append(tot / float(w) - zp[l * nb:(l + 1) * nb, cols])
        diffs.append(jnp.concatenate(pieces, axis=0))
    a_bf = _pool_project(diffs, wpool_ref, pscale_ref[...])

    zuv = _gelu(z[:, pw:])
    u = zuv[:, :pw]
    vn = _head_layernorm(zuv[:, pw:], lng_ref[...], lnb_ref[...], hd)
    vfull = jnp.concatenate(vn, axis=-1)
    for l in range(n_new):
        vn_ref[l] = vfull[l * nb:(l + 1) * nb, :]
    b_cols = []
    for hh in range(SGU_HEADS):
        pieces = []
        for l in range(n_new):
            mixed = None
            for l2 in range(l + 1):
                term = sw_ref[(hh * n_new + l) * n_new + l2] * vn[hh][l2 * nb:(l2 + 1) * nb, :]
                mixed = term if mixed is None else mixed + term
            mixed = mixed + sb_ref[hh * n_new + l]
            pieces.append(u[l * nb:(l + 1) * nb, hh * hd:(hh + 1) * hd] * mixed)
        b_cols.append(jnp.concatenate(pieces, axis=0))
    route(x, a_bf, jnp.concatenate(b_cols, axis=-1).astype(BF16))


def _mixer_body(xp_ref, xs_ref, st_ref, nm_ref, win_ref, wpool_ref, pscale_ref, lng_ref, lnb_ref,
                sw_ref, sbt_ref, sws_ref, sbs_ref, wout_ref, nf_ref, wrt_ref, br_ref,
                x1_ref, h2_ref, idx_ref, w_ref, rank_ref, cnt_out_ref, tail_ref, zp_ref, vn_ref,
                carry_ref, cnt_ref, bo_ref, *, tiles_per_prompt):
    i = pl.program_id(0)
    n_prompt_steps = pl.num_programs(0) - 1

    @pl.when(i == 0)
    def _():
        cnt_ref[...] = jnp.zeros_like(cnt_ref)

    route = functools.partial(_route_and_store, wout_ref=wout_ref, nf_ref=nf_ref, wrt_ref=wrt_ref,
                              br_ref=br_ref, cnt_ref=cnt_ref, x1_ref=x1_ref, h2_ref=h2_ref,
                              idx_ref=idx_ref,
                              w_ref=w_ref, rank_ref=rank_ref, cnt_out_ref=cnt_out_ref)

    @pl.when(i < n_prompt_steps)
    def _():
        _prompt_tile(lax.rem(i, tiles_per_prompt), xp_ref, nm_ref, win_ref, wpool_ref, pscale_ref,
                     lng_ref, lnb_ref, sw_ref, sbt_ref, tail_ref, carry_ref, bo_ref,
                     tiles_per_prompt, route)

    @pl.when(i == n_prompt_steps)
    def _():
        _sample_tile(xs_ref, st_ref, nm_ref, win_ref, wpool_ref, pscale_ref, lng_ref, lnb_ref,
                     sws_ref, sbs_ref, zp_ref, vn_ref, route)


def _row_copy(src, src_row, dst, dst_row, sem):
    s_rows = pl.ds(pl.multiple_of(src_row * SUBLANES, SUBLANES), SUBLANES)
    d_rows = pl.ds(pl.multiple_of(dst_row * SUBLANES, SUBLANES), SUBLANES)
    return pltpu.make_async_copy(src.at[s_rows], dst.at[d_rows], sem)


def _scatter_body(pstart_ref, pcnt_ref, nused_ref, dest_ref, x1_ref, xs_hbm, zero_ref, zsem, rsem):
    i = pl.program_id(0)
    ts = x1_ref.shape[0] // SUBLANES
    tm = zero_ref.shape[0] // SUBLANES
    n_exp = pstart_ref.shape[0]

    def zero_block(row):
        rows = pl.ds(pl.multiple_of(row * SUBLANES, SUBLANES), tm * SUBLANES)
        return pltpu.make_async_copy(zero_ref, xs_hbm.at[rows], zsem)

    @pl.when(i == 0)
    def _():
        zero_ref[...] = jnp.zeros_like(zero_ref)
        for e in range(n_exp):
            zero_block(pstart_ref[e] + pcnt_ref[e]).start()
        for e in range(n_exp):
            zero_block(pstart_ref[e] + pcnt_ref[e]).wait()

        n_total = xs_hbm.shape[0] // (tm * SUBLANES)

        def start_tail(b, c):
            zero_block(b * tm).start()
            return c

        def wait_tail(b, c):
            zero_block(b * tm).wait()
            return c

        lax.fori_loop(nused_ref[0], n_total, start_tail, 0)
        lax.fori_loop(nused_ref[0], n_total, wait_tail, 0)

    def start_rows(r, carry):
        for k in range(TOP_K):
            _row_copy(x1_ref, r, xs_hbm, dest_ref[0, 0, k * ts + r], rsem).start(priority=k % DMA_PRIORITIES)
        return carry

    lax.fori_loop(0, ts, start_rows, 0, unroll=ROW_DMA_UNROLL)

    def wait_rows(r, carry):
        for k in range(TOP_K):
            _row_copy(x1_ref, 0, xs_hbm, 0, rsem).wait()
        return carry

    lax.fori_loop(0, ts, wait_rows, 0, unroll=ROW_DMA_UNROLL)


def _experts_body(be_ref, nused_ref, eord_ref, eorder_ref, ndist_ref, bvalid_ref,
                  xs_ref, wup_hbm, bup_ref, wdn_hbm, bdn_ref, ys_ref,
                  wup_f32, wdn_f32, wsem):
    b = pl.program_id(0)
    d_ff = wdn_f32.shape[1]

    def weight_copies(e, slot):
        return (pltpu.make_async_copy(wup_hbm.at[e], wup_f32.at[slot], wsem.at[slot, 0]),
                pltpu.make_async_copy(wdn_hbm.at[e], wdn_f32.at[slot], wsem.at[slot, 1]))

    @pl.when(b == 0)
    def _():
        for cp in weight_copies(eorder_ref[0], 0):
            cp.start()

    @pl.when(b < nused_ref[0])
    def _():
        prev = be_ref[jnp.maximum(b - 1, 0)]

        @pl.when((b == 0) | (be_ref[b] != prev))
        def _():
            j = eord_ref[b]
            slot = lax.rem(j, WEIGHT_SLOTS)
            for cp in weight_copies(be_ref[b], slot):
                cp.wait()

            @pl.when(j + 1 < ndist_ref[0])
            def _():
                for cp in weight_copies(eorder_ref[j + 1], lax.rem(j + 1, WEIGHT_SLOTS)):
                    cp.start()

        wslot = lax.rem(eord_ref[b], WEIGHT_SLOTS)

        def compute(n_rows):
            xb = _load_rows(xs_ref, n_rows).astype(BF16)
            acc = None
            for j in range(0, d_ff, FF_CHUNK):
                glu = _dot(xb, wup_f32[wslot, :, j:j + FF_CHUNK].astype(BF16)) + bup_ref[0, :, j:j + FF_CHUNK]
                lin = (_dot(xb, wup_f32[wslot, :, d_ff + j:d_ff + j + FF_CHUNK].astype(BF16))
                       + bup_ref[0, :, d_ff + j:d_ff + j + FF_CHUNK])
                glu = jnp.minimum(glu, SWIGLU_LIMIT)
                lin = jnp.clip(lin, -SWIGLU_LIMIT, SWIGLU_LIMIT)
                act = glu * jax.nn.sigmoid(SWIGLU_ALPHA * glu) * (lin + 1.0)
                part = _dot(act.astype(BF16), wdn_f32[wslot, j:j + FF_CHUNK, :].astype(BF16))
                acc = part if acc is None else acc + part
            _store_rows(ys_ref, acc + bdn_ref[0])
            if n_rows * SUBLANES < ys_ref.shape[0]:
                ys_ref[n_rows * SUBLANES:, :] = jnp.zeros(
                    (ys_ref.shape[0] - n_rows * SUBLANES, ys_ref.shape[1]), ys_ref.dtype)

        tm = ys_ref.shape[0] // SUBLANES
        step = tm // ROW_VARIANTS
        valid = bvalid_ref[b]
        for hi in range(step, tm + 1, step):
            lo = hi - step if hi > step else -1
            @pl.when((valid > lo) & (valid <= hi))
            def _(hi=hi):
                compute(hi)

    @pl.when(b >= nused_ref[0])
    def _():
        ys_ref[...] = jnp.zeros_like(ys_ref)


def _combine_body(dcur_ref, dnext_ref, wsm_ref, ys_hbm, x1_ref, pp_ref, ps_ref, wg_ref, wp_ref, nfin_ref,
                  yp_ref, ysm_ref, buf, acc_ref, sem, *, n_prompt_tiles):
    i = pl.program_id(0)
    n = pl.num_programs(0)
    ts = x1_ref.shape[0] // SUBLANES
    slot = i % 2

    def issue_row(dref, sl, r):
        for k in range(TOP_K):
            cp = _row_copy(ys_hbm, dref[0, 0, k * ts + r], buf.at[sl, k], r, sem.at[sl])
            cp.start(priority=k % DMA_PRIORITIES)

    @pl.when(i == 0)
    def _():
        def body(r, c):
            issue_row(dcur_ref, 0, r)
            return c
        lax.fori_loop(0, ts, body, 0, unroll=ROW_DMA_UNROLL)

    def wait_rows(r, c):
        for k in range(TOP_K):
            _row_copy(ys_hbm, 0, buf.at[slot, k], 0, sem.at[slot]).wait()
        return c

    lax.fori_loop(0, ts, wait_rows, 0, unroll=ROW_DMA_UNROLL)

    def sum_rows(start_next):
        def body(r, c):
            rows = pl.ds(pl.multiple_of(r * SUBLANES, SUBLANES), SUBLANES)
            acc = x1_ref[rows, :]
            for k in range(TOP_K):
                acc = acc + buf[slot, k, rows, :] * wsm_ref[0, 0, k * ts + r]
            acc_ref[rows, :] = acc
            if start_next:
                issue_row(dnext_ref, 1 - slot, r)
            return c
        lax.fori_loop(0, ts, body, 0, unroll=ROW_DMA_UNROLL)

    @pl.when(i + 1 < n)
    def _():
        sum_rows(True)

    @pl.when(i + 1 == n)
    def _():
        sum_rows(False)

    acc = _load_rows(acc_ref, ts)
    gate = jax.nn.sigmoid(_dot(acc.astype(BF16), wg_ref[...]))
    p = jnp.where(i < n_prompt_tiles, pp_ref[...], ps_ref[...])
    proj = _dot(p.astype(BF16), wp_ref[...])
    y = _rms(acc + gate * proj, nfin_ref[...])

    @pl.when(i < n_prompt_tiles)
    def _():
        yp_ref[...] = y

    @pl.when(i >= n_prompt_tiles)
    def _():
        ysm_ref[...] = y


def _const_spec(shape):
    nd = len(shape)
    return pl.BlockSpec(shape, lambda *_: (0,) * nd)


def kernel(x_prompt, x_sample, state_pool, p_prompt, p_sample, norm_mix, w_in, w_pool, pool_scale,
           sgu_ln_g, sgu_ln_b, sgu_w, sgu_b, w_out, norm_ffn, w_router, b_router, w_up, b_up,
           w_down, b_down, w_ple_gate, w_ple_proj, norm_final):
    B, L, D = x_prompt.shape
    NB, NL, _ = x_sample.shape
    PW = state_pool.shape[-1]
    NE = w_router.shape[-1]
    DFF = w_down.shape[2]
    PLE = p_prompt.shape[-1]
    TP = B * L
    TSM = NB * NL
    T = TP + TSM
    assert w_in.shape[0] == 1, "single trunk layer"
    assert L % TS_MIX == 0 and TSM == TS_MIX and TS_MIX % CHUNK == 0
    assert T % TS_SCATTER == 0 and TP % TS_COMBINE == 0 and TSM % TS_COMBINE == 0
    assert D == SUBLANES * LANES, "a token row must fill exactly one vreg tile"
    assert state_pool.shape[2] == POOL_BUF and PW % len(POOL_WINDOWS) == 0 and PW % SGU_HEADS == 0

    nm = norm_mix[0].reshape(1, D)
    nf = norm_ffn[0].reshape(1, D)
    nfin = norm_final.reshape(1, D)
    win_bf = w_in[0].astype(BF16)
    wpool_bf = w_pool[0].astype(BF16)
    pscale = pool_scale[0].reshape(1, PW)
    lng = sgu_ln_g[0].reshape(1, PW)
    lnb = sgu_ln_b[0].reshape(1, PW)
    wout_bf = w_out[0].astype(BF16)
    wrt = w_router[0].T
    br = b_router[0].reshape(NE, 1)
    wg_bf = w_ple_gate[0].astype(BF16)
    wp_bf = w_ple_proj[0].astype(BF16)

    cparams = functools.partial(pltpu.CompilerParams, vmem_limit_bytes=VMEM_LIMIT_BYTES)
    NT = L // TS_MIX
    n_prompt_steps = B * NT
    xs_rows = x_sample.transpose(1, 0, 2).reshape(TSM, D)
    hist = state_pool[0].transpose(1, 0, 2)
    sw_small = sgu_w[0, :, :NL, :NL].reshape(-1)
    sb_small = sgu_b[0, :, :NL].reshape(-1)
    smem = pl.BlockSpec(memory_space=pltpu.SMEM)
    any_spec = pl.BlockSpec(memory_space=pl.ANY)
    last_prompt = n_prompt_steps - 1
    x1_all, h2_all, idx_t, w_t, rank_t, counts, pool_tail, zp_s, vn_s = pl.pallas_call(
        functools.partial(_mixer_body, tiles_per_prompt=NT),
        grid=(n_prompt_steps + 1,),
        in_specs=[
            pl.BlockSpec((TS_MIX, D), lambda i: (jnp.minimum(i, last_prompt), 0)),
            _const_spec((TSM, D)), _const_spec(hist.shape),
            _const_spec((1, D)), _const_spec(win_bf.shape), _const_spec(wpool_bf.shape),
            _const_spec((1, PW)), _const_spec((1, PW)), _const_spec((1, PW)),
            _const_spec(sgu_w[0].shape), _const_spec((CHUNK, SGU_HEADS)), smem, smem,
            _const_spec(wout_bf.shape), _const_spec((1, D)), _const_spec(wrt.shape),
            _const_spec((NE, 1)),
        ],
        out_specs=[
            pl.BlockSpec((TS_MIX * SUBLANES, LANES), lambda i: (i, 0)),
            pl.BlockSpec((TS_MIX * SUBLANES, LANES), lambda i: (i, 0)),
            pl.BlockSpec((TOP_K, TS_MIX), lambda i: (0, i)),
            pl.BlockSpec((TOP_K, TS_MIX), lambda i: (0, i)),
            pl.BlockSpec((TOP_K, TS_MIX), lambda i: (0, i)),
            _const_spec((NE, LANES)),
            pl.BlockSpec((1, CARRY_ROWS, PW), lambda i: (jnp.minimum(i, last_prompt) // NT, 0, 0)),
            _const_spec((NL, NB, PW)),
            _const_spec((NL, NB, PW)),
        ],
        out_shape=[
            jax.ShapeDtypeStruct((T * SUBLANES, LANES), F32),
            jax.ShapeDtypeStruct((T * SUBLANES, LANES), F32),
            jax.ShapeDtypeStruct((TOP_K, T), I32),
            jax.ShapeDtypeStruct((TOP_K, T), F32),
            jax.ShapeDtypeStruct((TOP_K, T), I32),
            jax.ShapeDtypeStruct((NE, LANES), F32),
            jax.ShapeDtypeStruct((B, CARRY_ROWS, PW), F32),
            jax.ShapeDtypeStruct((NL, NB, PW), F32),
            jax.ShapeDtypeStruct((NL, NB, PW), F32),
        ],
        scratch_shapes=[pltpu.VMEM((CARRY_ROWS, PW), F32), pltpu.VMEM((NE, LANES), F32),
                        pltpu.VMEM((TS_MIX, PW), BF16)],
        compiler_params=cparams(dimension_semantics=("arbitrary",)),
        name="mixer",
    )(x_prompt.reshape(TP, D), xs_rows, hist, nm, win_bf, wpool_bf, pscale, lng, lnb, sgu_w[0],
      sgu_b[0].T, sw_small, sb_small, wout_bf, nf, wrt, br)

    A = T * TOP_K
    n_blocks = -(-A // TM_MOE) + NE
    n_slots = n_blocks * TM_MOE
    cnt_i = counts[:, 0].astype(I32)
    padded = (cnt_i + TM_MOE - 1) // TM_MOE * TM_MOE
    pad_end = jnp.cumsum(padded)
    pad_start = pad_end - padded
    n_used = (pad_end[-1] // TM_MOE).astype(I32).reshape(1)
    blk = jnp.minimum(jnp.arange(n_blocks, dtype=I32), n_used[0] - 1)
    block_e = jnp.minimum(jnp.sum(pad_end[None, :] <= (blk * TM_MOE)[:, None], axis=1), NE - 1).astype(I32)
    expert_ids = jnp.arange(NE, dtype=I32)[:, None, None]
    dest = jnp.sum(jnp.where(idx_t[None] == expert_ids, pad_start[:, None, None], 0), axis=0) + rank_t
    used = padded > 0
    ordinal = jnp.cumsum(used.astype(I32)) - 1
    n_distinct = jnp.sum(used.astype(I32)).reshape(1)
    block_ord = jnp.sum(jnp.where(block_e[:, None] == expert_ids[:, 0, 0][None, :], ordinal[None, :], 0), axis=1)
    slots_j = jnp.minimum(jnp.arange(NE + 1, dtype=I32), n_distinct[0] - 1)
    expert_order = jnp.sum(jnp.where(used[None, :] & (ordinal[None, :] == slots_j[:, None]),
                                     expert_ids[:, 0, 0][None, :], 0), axis=1).astype(I32)

    onehot_be = block_e[:, None] == expert_ids[:, 0, 0][None, :]
    first_row = jnp.sum(jnp.where(onehot_be, pad_start[None, :], 0), axis=1)
    rows_left = jnp.sum(jnp.where(onehot_be, cnt_i[None, :], 0), axis=1) - (blk * TM_MOE - first_row)
    block_valid = jnp.where(jnp.arange(n_blocks, dtype=I32) < n_used[0],
                            jnp.clip(rows_left, 0, TM_MOE), 0).astype(I32)

    def per_tile(a, ts):
        return a.reshape(TOP_K, T // ts, ts).transpose(1, 0, 2).reshape(T // ts, 1, TOP_K * ts)

    xs_sorted = pl.pallas_call(
        _scatter_body,
        grid_spec=pltpu.PrefetchScalarGridSpec(
            num_scalar_prefetch=3,
            grid=(T // TS_SCATTER,),
            in_specs=[pl.BlockSpec((1, 1, TOP_K * TS_SCATTER), lambda i, *_: (i, 0, 0),
                                   memory_space=pltpu.SMEM),
                      pl.BlockSpec((TS_SCATTER * SUBLANES, LANES), lambda i, *_: (i, 0))],
            out_specs=any_spec,
            scratch_shapes=[pltpu.VMEM((TM_MOE * SUBLANES, LANES), F32), pltpu.SemaphoreType.DMA(()),
                            pltpu.SemaphoreType.DMA(())],
        ),
        out_shape=jax.ShapeDtypeStruct(((n_slots + TM_MOE) * SUBLANES, LANES), F32),
        compiler_params=cparams(dimension_semantics=("arbitrary",)),
        name="moe_scatter",
    )(pad_start, cnt_i, n_used, per_tile(dest, TS_SCATTER), h2_all)

    ys_sorted = pl.pallas_call(
        _experts_body,
        grid_spec=pltpu.PrefetchScalarGridSpec(
            num_scalar_prefetch=6,
            grid=(n_blocks,),
            in_specs=[
                pl.BlockSpec((TM_MOE * SUBLANES, LANES), lambda b, be, nu, *_: (jnp.minimum(b, nu[0] - 1), 0)),
                any_spec,
                pl.BlockSpec((1, 1, 2 * DFF), lambda b, be, *_: (be[b], 0, 0)),
                any_spec,
                pl.BlockSpec((1, 1, D), lambda b, be, *_: (be[b], 0, 0)),
            ],
            out_specs=pl.BlockSpec((TM_MOE * SUBLANES, LANES), lambda b, *_: (b, 0)),
            scratch_shapes=[pltpu.VMEM((WEIGHT_SLOTS, D, 2 * DFF), F32), pltpu.VMEM((WEIGHT_SLOTS, DFF, D), F32),
                            pltpu.SemaphoreType.DMA((WEIGHT_SLOTS, 2))],
        ),
        out_shape=jax.ShapeDtypeStruct((n_slots * SUBLANES, LANES), F32),
        compiler_params=cparams(dimension_semantics=("arbitrary",)),
        name="moe_experts",
    )(block_e, n_used, block_ord.astype(I32), expert_order, n_distinct, block_valid, xs_sorted, w_up[0],
      b_up[0].reshape(NE, 1, 2 * DFF), w_down[0], b_down[0].reshape(NE, 1, D))

    n_tiles = T // TS_COMBINE
    npt = TP // TS_COMBINE
    y_p, y_s = pl.pallas_call(
        functools.partial(_combine_body, n_prompt_tiles=npt),
        grid=(n_tiles,),
        in_specs=[
            pl.BlockSpec((1, 1, TOP_K * TS_COMBINE), lambda i: (i, 0, 0), memory_space=pltpu.SMEM),
            pl.BlockSpec((1, 1, TOP_K * TS_COMBINE), lambda i: (jnp.minimum(i + 1, n_tiles - 1), 0, 0),
                         memory_space=pltpu.SMEM),
            pl.BlockSpec((1, 1, TOP_K * TS_COMBINE), lambda i: (i, 0, 0), memory_space=pltpu.SMEM),
            any_spec,
            pl.BlockSpec((TS_COMBINE * SUBLANES, LANES), lambda i: (i, 0)),
            pl.BlockSpec((TS_COMBINE, PLE), lambda i: (jnp.minimum(i, npt - 1), 0)),
            pl.BlockSpec((TS_COMBINE, PLE), lambda i: (jnp.maximum(i - npt, 0), 0)),
            _const_spec(wg_bf.shape), _const_spec(wp_bf.shape), _const_spec((1, D)),
        ],
        out_specs=[
            pl.BlockSpec((TS_COMBINE, D), lambda i: (jnp.minimum(i, npt - 1), 0)),
            pl.BlockSpec((TS_COMBINE, D), lambda i: (jnp.maximum(i - npt, 0), 0)),
        ],
        out_shape=[jax.ShapeDtypeStruct((TP, D), F32), jax.ShapeDtypeStruct((TSM, D), F32)],
        scratch_shapes=[pltpu.VMEM((2, TOP_K, TS_COMBINE * SUBLANES, LANES), F32),
                        pltpu.VMEM((TS_COMBINE * SUBLANES, LANES), F32),
                        pltpu.SemaphoreType.DMA((2,))],
        compiler_params=cparams(dimension_semantics=("arbitrary",)),
        name="moe_combine",
    )(per_tile(dest, TS_COMBINE), per_tile(dest, TS_COMBINE), per_tile(w_t, TS_COMBINE), ys_sorted,
      x1_all, p_prompt[0].reshape(TP, PLE), p_sample[0].transpose(1, 0, 2).reshape(TSM, PLE),
      wg_bf, wp_bf, nfin)

    y_prompt = y_p.reshape(B, L, D)
    y_sample = y_s.reshape(NL, NB, D).transpose(1, 0, 2)
    new_pool_prompt = pool_tail[:, CARRY_ROWS - POOL_BUF:, :][None]
    new_pool_sample = jnp.concatenate([state_pool[0][:, NL:, :], zp_s.transpose(1, 0, 2)], axis=1)[None]
    new_sgu_v_sample = vn_s.transpose(1, 0, 2)[None]
    return (y_prompt, y_sample, new_pool_prompt, new_pool_sample, new_sgu_v_sample)
```

```python
import functools

import jax
import jax.numpy as jnp
from jax import lax
from jax.experimental import pallas as pl
from jax.experimental.pallas import tpu as pltpu

F32 = jnp.float32
BF16 = jnp.bfloat16
I32 = jnp.int32

POOL_WINDOWS = (2, 4, 8, 16)
POOL_BUF = max(POOL_WINDOWS) - 1
SGU_HEADS = 4
CHUNK = 128
TOP_K = 4
SWIGLU_LIMIT = 7.0
SWIGLU_ALPHA = 1.702
RMS_EPS = 1e-6
LN_EPS = 1e-5

SUBLANES = 8
LANES = 128
CARRY_ROWS = 16
TS_MIX = 512
TS_SCATTER = 512
DMA_PRIORITIES = 2
ROW_DMA_UNROLL = 8
WEIGHT_SLOTS = 2
TM_MOE = 512
FF_CHUNK = 512
ROW_VARIANTS = 4
TS_COMBINE = 512
VMEM_LIMIT_BYTES = 56 * 1024 * 1024


def _rms(x, g):
    return x * lax.rsqrt(jnp.mean(x * x, axis=-1, keepdims=True) + RMS_EPS) * g


def _dot(a, b):
    return jnp.dot(a, b, preferred_element_type=F32)


def _load_rows(ref, n_rows):
    return jnp.concatenate([ref[pl.ds(c, n_rows, stride=SUBLANES), :] for c in range(SUBLANES)], axis=-1)


def _store_rows(ref, val):
    n_rows = val.shape[0]
    for c in range(SUBLANES):
        ref[pl.ds(c, n_rows, stride=SUBLANES), :] = val[:, c * LANES:(c + 1) * LANES]


def _gelu(x):
    return 0.5 * x * (1.0 + lax.erf(x * (2.0 ** -0.5)))


def _head_layernorm(v, g, b, hd):
    outs = []
    for h in range(SGU_HEADS):
        vh = v[:, h * hd:(h + 1) * hd]
        mu = jnp.mean(vh, axis=-1, keepdims=True)
        d = vh - mu
        var = jnp.mean(d * d, axis=-1, keepdims=True)
        outs.append(d * lax.rsqrt(var + LN_EPS) * g[:, h * hd:(h + 1) * hd] + b[:, h * hd:(h + 1) * hd])
    return outs


def _pool_project(diffs, wpool_ref, pscale):
    gd = diffs[0].shape[-1]
    outs = []
    for g, d in enumerate(diffs):
        outs.append(_dot(d.astype(BF16), wpool_ref[g]) * pscale[:, g * gd:(g + 1) * gd])
    return jnp.concatenate(outs, axis=-1).astype(BF16)


def _route_and_store(x, a_bf, b_bf, wout_ref, nf_ref, wrs_ref, br_ref, cnt_ref,
                     x1_ref, h2_ref, idx_ref, w_ref, rank_ref, cnt_out_ref):
    ts = x.shape[0]
    pw = a_bf.shape[-1]
    n_exp = wrs_ref.shape[0] // 2
    mix = _dot(a_bf, wout_ref[:pw, :]) + _dot(b_bf, wout_ref[pw:, :])
    x1 = x + mix
    _store_rows(x1_ref, x1)
    h2 = _rms(x1, nf_ref[...])
    _store_rows(h2_ref, h2)
    h_hi = h2.astype(BF16)
    h_lo = (h2 - h_hi.astype(F32)).astype(BF16)
    nt = (((1,), (1,)), ((), ()))
    part = lax.dot_general(wrs_ref[...], h_hi, nt, preferred_element_type=F32)
    logits = (part[:n_exp] + part[n_exp:]
              + lax.dot_general(wrs_ref[:n_exp, :], h_lo, nt, preferred_element_type=F32) + br_ref[...])
    eio = lax.broadcasted_iota(I32, (n_exp, ts), 0).astype(F32)
    l = logits
    tops, sels = [], []
    for _ in range(TOP_K):
        m = jnp.max(l, axis=0, keepdims=True)
        sel = jnp.min(jnp.where(l == m, eio, float(n_exp)), axis=0, keepdims=True)
        tops.append(m)
        sels.append(sel)
        l = jnp.where(eio == sel, -jnp.inf, l)
    exps = [jnp.exp(t - tops[0]) for t in tops]
    denom = exps[0] + exps[1] + exps[2] + exps[3]
    onehot = jnp.zeros((n_exp, ts), F32)
    for sel in sels:
        onehot = onehot + jnp.where(eio == sel, 1.0, 0.0)
    r_i = lax.broadcasted_iota(I32, (ts, ts), 0)
    c_i = lax.broadcasted_iota(I32, (ts, ts), 1)
    upper = jnp.where(r_i < c_i, 1.0, 0.0).astype(BF16)
    base = cnt_ref[:, 0:1]
    rank_all = _dot(onehot.astype(BF16), upper) + base
    for k in range(TOP_K):
        idx_ref[k:k + 1, :] = sels[k].astype(I32)
        w_ref[k:k + 1, :] = exps[k] / denom
        rk = jnp.sum(jnp.where(eio == sels[k], rank_all, 0.0), axis=0, keepdims=True)
        rank_ref[k:k + 1, :] = rk.astype(I32)
    cnt_ref[...] = cnt_ref[...] + jnp.sum(onehot, axis=1, keepdims=True)
    cnt_out_ref[...] = cnt_ref[...]


def _prompt_tile(t, x_ref, nm_ref, win_ref, wpool_ref, pscale_ref, lng_ref, lnb_ref, sw_ref, sbt_ref,
                 tail_ref, carry_ref, bo_ref, n_tiles, route):
    ts = x_ref.shape[0]
    pw = pscale_ref.shape[-1]
    gd = pw // len(POOL_WINDOWS)
    hd = pw // SGU_HEADS

    @pl.when(t == 0)
    def _():
        carry_ref[...] = jnp.zeros_like(carry_ref)

    x = x_ref[...]
    h = _rms(x, nm_ref[...])
    z = _dot(h.astype(BF16), win_ref[...])
    zp = z[:, :pw]

    ext = jnp.concatenate([carry_ref[...], zp], axis=0)
    carry_ref[...] = zp[ts - CARRY_ROWS:, :]

    @pl.when(t == n_tiles - 1)
    def _():
        tail_ref[0] = zp[ts - CARRY_ROWS:, :]

    pos = t * ts + lax.broadcasted_iota(I32, (ts, 1), 0)
    diffs = []
    for g, w in enumerate(POOL_WINDOWS):
        acc = ext[:, g * gd:(g + 1) * gd]
        k = 1
        while k < w:
            acc = acc + pltpu.roll(acc, k, 0)
            k *= 2
        cnt = jnp.minimum(w, pos + 1).astype(F32)
        diffs.append(acc[CARRY_ROWS:, :] / cnt - zp[:, g * gd:(g + 1) * gd])
    a_bf = _pool_project(diffs, wpool_ref, pscale_ref[...])

    zuv = _gelu(z[:, pw:])
    u = zuv[:, :pw]
    vn = _head_layernorm(zuv[:, pw:], lng_ref[...], lnb_ref[...], hd)
    ri = lax.broadcasted_iota(I32, (CHUNK, CHUNK), 0)
    ci = lax.broadcasted_iota(I32, (CHUNK, CHUNK), 1)
    for hh in range(SGU_HEADS):
        wm = jnp.where(ci <= ri, sw_ref[hh], 0.0).astype(BF16)
        bias = sbt_ref[:, hh:hh + 1]
        vh = vn[hh].astype(BF16)
        for c in range(ts // CHUNK):
            rows = slice(c * CHUNK, (c + 1) * CHUNK)
            mixed = _dot(wm, vh[rows, :]) + bias
            bo_ref[rows, hh * hd:(hh + 1) * hd] = (u[rows, hh * hd:(hh + 1) * hd] * mixed).astype(BF16)
    route(x, a_bf, bo_ref[...])


def _sample_tile(x_ref, st_ref, nm_ref, win_ref, wpool_ref, pscale_ref, lng_ref, lnb_ref, sw_ref,
                 sb_ref, zp_ref, vn_ref, route):
    n_new, nb, pw = zp_ref.shape
    gd = pw // len(POOL_WINDOWS)
    hd = pw // SGU_HEADS

    x = x_ref[...]
    h = _rms(x, nm_ref[...])
    z = _dot(h.astype(BF16), win_ref[...])
    zp = z[:, :pw]
    for l in range(n_new):
        zp_ref[l] = zp[l * nb:(l + 1) * nb, :]

    diffs = []
    for g, w in enumerate(POOL_WINDOWS):
        cols = slice(g * gd, (g + 1) * gd)
        lo_needed = POOL_BUF + 1 - w
        suffix = {}
        run = None
        for r in range(POOL_BUF - 1, lo_needed - 1, -1):
            slab = st_ref[r][:, cols]
            run = slab if run is None else run + slab
            suffix[r] = run
        pieces = []
        for l in range(n_new):
            first_hist = POOL_BUF + 1 + l - w
            tot = suffix[first_hist] if first_hist < POOL_BUF else None
            for l2 in range(max(0, l - w + 1), l + 1):
                zl = zp[l2 * nb:(l2 + 1) * nb, cols]
                tot = zl if tot is None else tot + zl
            pieces.append(tot / float(w) - zp[l * nb:(l + 1) * nb, cols])
        diffs.append(jnp.concatenate(pieces, axis=0))
    a_bf = _pool_project(diffs, wpool_ref, pscale_ref[...])

    zuv = _gelu(z[:, pw:])
    u = zuv[:, :pw]
    vn = _head_layernorm(zuv[:, pw:], lng_ref[...], lnb_ref[...], hd)
    vfull = jnp.concatenate(vn, axis=-1)
    for l in range(n_new):
        vn_ref[l] = vfull[l * nb:(l + 1) * nb, :]
    b_cols = []
    for hh in range(SGU_HEADS):
        pieces = []
        for l in range(n_new):
            mixed = None
            for l2 in range(l + 1):
                term = sw_ref[(hh * n_new + l) * n_new + l2] * vn[hh][l2 * nb:(l2 + 1) * nb, :]
                mixed = term if mixed is None else mixed + term
            mixed = mixed + sb_ref[hh * n_new + l]
            pieces.append(u[l * nb:(l + 1) * nb, hh * hd:(hh + 1) * hd] * mixed)
        b_cols.append(jnp.concatenate(pieces, axis=0))
    route(x, a_bf, jnp.concatenate(b_cols, axis=-1).astype(BF16))


def _mixer_body(xp_ref, xs_ref, st_ref, nm_ref, win_ref, wpool_ref, pscale_ref, lng_ref, lnb_ref,
                sw_ref, sbt_ref, sws_ref, sbs_ref, wout_ref, nf_ref, wrs_ref, br_ref,
                x1_ref, h2_ref, idx_ref, w_ref, rank_ref, cnt_out_ref, tail_ref, zp_ref, vn_ref,
                carry_ref, cnt_ref, bo_ref, *, tiles_per_prompt):
    i = pl.program_id(0)
    n_prompt_steps = pl.num_programs(0) - 1

    @pl.when(i == 0)
    def _():
        cnt_ref[...] = jnp.zeros_like(cnt_ref)

    route = functools.partial(_route_and_store, wout_ref=wout_ref, nf_ref=nf_ref, wrs_ref=wrs_ref,
                              br_ref=br_ref, cnt_ref=cnt_ref, x1_ref=x1_ref, h2_ref=h2_ref,
                              idx_ref=idx_ref,
                              w_ref=w_ref, rank_ref=rank_ref, cnt_out_ref=cnt_out_ref)

    @pl.when(i < n_prompt_steps)
    def _():
        _prompt_tile(lax.rem(i, tiles_per_prompt), xp_ref, nm_ref, win_ref, wpool_ref, pscale_ref,
                     lng_ref, lnb_ref, sw_ref, sbt_ref, tail_ref, carry_ref, bo_ref,
                     tiles_per_prompt, route)

    @pl.when(i == n_prompt_steps)
    def _():
        _sample_tile(xs_ref, st_ref, nm_ref, win_ref, wpool_ref, pscale_ref, lng_ref, lnb_ref,
                     sws_ref, sbs_ref, zp_ref, vn_ref, route)


def _row_copy(src, src_row, dst, dst_row, sem):
    s_rows = pl.ds(pl.multiple_of(src_row * SUBLANES, SUBLANES), SUBLANES)
    d_rows = pl.ds(pl.multiple_of(dst_row * SUBLANES, SUBLANES), SUBLANES)
    return pltpu.make_async_copy(src.at[s_rows], dst.at[d_rows], sem)


def _scatter_body(pstart_ref, pcnt_ref, nused_ref, dest_ref, x1_ref, xs_hbm, zero_ref, zsem, rsem):
    i = pl.program_id(0)
    ts = x1_ref.shape[0] // SUBLANES
    tm = zero_ref.shape[0] // SUBLANES
    n_exp = pstart_ref.shape[0]

    def zero_block(row):
        rows = pl.ds(pl.multiple_of(row * SUBLANES, SUBLANES), tm * SUBLANES)
        return pltpu.make_async_copy(zero_ref, xs_hbm.at[rows], zsem)

    @pl.when(i == 0)
    def _():
        zero_ref[...] = jnp.zeros_like(zero_ref)
        for e in range(n_exp):
            zero_block(pstart_ref[e] + pcnt_ref[e]).start()
        for e in range(n_exp):
            zero_block(pstart_ref[e] + pcnt_ref[e]).wait()

        n_total = xs_hbm.shape[0] // (tm * SUBLANES)

        def start_tail(b, c):
            zero_block(b * tm).start()
            return c

        def wait_tail(b, c):
            zero_block(b * tm).wait()
            return c

        lax.fori_loop(nused_ref[0], n_total, start_tail, 0)
        lax.fori_loop(nused_ref[0], n_total, wait_tail, 0)

    def start_rows(r, carry):
        for k in range(TOP_K):
            _row_copy(x1_ref, r, xs_hbm, dest_ref[0, 0, k * ts + r], rsem).start(priority=k % DMA_PRIORITIES)
        return carry

    lax.fori_loop(0, ts, start_rows, 0, unroll=ROW_DMA_UNROLL)

    def wait_rows(r, carry):
        for k in range(TOP_K):
            _row_copy(x1_ref, 0, xs_hbm, 0, rsem).wait()
        return carry

    lax.fori_loop(0, ts, wait_rows, 0, unroll=ROW_DMA_UNROLL)


def _experts_body(be_ref, nused_ref, eord_ref, eorder_ref, ndist_ref, bvalid_ref,
                  xs_ref, wup_hbm, bup_ref, wdn_hbm, bdn_ref, ys_ref,
                  wup_f32, wdn_f32, wsem):
    b = pl.program_id(0)
    d_ff = wdn_f32.shape[1]

    def weight_copies(e, slot):
        return (pltpu.make_async_copy(wup_hbm.at[e], wup_f32.at[slot], wsem.at[slot, 0]),
                pltpu.make_async_copy(wdn_hbm.at[e], wdn_f32.at[slot], wsem.at[slot, 1]))

    @pl.when(b == 0)
    def _():
        for cp in weight_copies(eorder_ref[0], 0):
            cp.start()

    @pl.when(b < nused_ref[0])
    def _():
        prev = be_ref[jnp.maximum(b - 1, 0)]

        @pl.when((b == 0) | (be_ref[b] != prev))
        def _():
            j = eord_ref[b]
            slot = lax.rem(j, WEIGHT_SLOTS)
            for cp in weight_copies(be_ref[b], slot):
                cp.wait()

            @pl.when(j + 1 < ndist_ref[0])
            def _():
                for cp in weight_copies(eorder_ref[j + 1], lax.rem(j + 1, WEIGHT_SLOTS)):
                    cp.start()

        wslot = lax.rem(eord_ref[b], WEIGHT_SLOTS)

        def compute(n_rows):
            xb = _load_rows(xs_ref, n_rows).astype(BF16)
            acc = None
            for j in range(0, d_ff, FF_CHUNK):
                glu = _dot(xb, wup_f32[wslot, :, j:j + FF_CHUNK].astype(BF16)) + bup_ref[0, :, j:j + FF_CHUNK]
                lin = (_dot(xb, wup_f32[wslot, :, d_ff + j:d_ff + j + FF_CHUNK].astype(BF16))
                       + bup_ref[0, :, d_ff + j:d_ff + j + FF_CHUNK])
                glu = jnp.minimum(glu, SWIGLU_LIMIT)
                lin = jnp.clip(lin, -SWIGLU_LIMIT, SWIGLU_LIMIT)
                act = glu * jax.nn.sigmoid(SWIGLU_ALPHA * glu) * (lin + 1.0)
                part = _dot(act.astype(BF16), wdn_f32[wslot, j:j + FF_CHUNK, :].astype(BF16))
                acc = part if acc is None else acc + part
            _store_rows(ys_ref, acc + bdn_ref[0])
            if n_rows * SUBLANES < ys_ref.shape[0]:
                ys_ref[n_rows * SUBLANES:, :] = jnp.zeros(
                    (ys_ref.shape[0] - n_rows * SUBLANES, ys_ref.shape[1]), ys_ref.dtype)

        tm = ys_ref.shape[0] // SUBLANES
        step = tm // ROW_VARIANTS
        valid = bvalid_ref[b]
        for hi in range(step, tm + 1, step):
            lo = hi - step if hi > step else -1
            @pl.when((valid > lo) & (valid <= hi))
            def _(hi=hi):
                compute(hi)

    @pl.when(b >= nused_ref[0])
    def _():
        ys_ref[...] = jnp.zeros_like(ys_ref)


def _combine_body(dcur_ref, dnext_ref, wsm_ref, ys_hbm, x1_ref, pp_ref, ps_ref, wg_ref, wp_ref, nfin_ref,
                  yp_ref, ysm_ref, buf, acc_ref, sem, *, n_prompt_tiles):
    i = pl.program_id(0)
    n = pl.num_programs(0)
    ts = x1_ref.shape[0] // SUBLANES
    slot = i % 2

    def issue_row(dref, sl, r):
        for k in range(TOP_K):
            cp = _row_copy(ys_hbm, dref[0, 0, k * ts + r], buf.at[sl, k], r, sem.at[sl])
            cp.start(priority=k % DMA_PRIORITIES)

    @pl.when(i == 0)
    def _():
        def body(r, c):
            issue_row(dcur_ref, 0, r)
            return c
        lax.fori_loop(0, ts, body, 0, unroll=ROW_DMA_UNROLL)

    def wait_rows(r, c):
        for k in range(TOP_K):
            _row_copy(ys_hbm, 0, buf.at[slot, k], 0, sem.at[slot]).wait()
        return c

    lax.fori_loop(0, ts, wait_rows, 0, unroll=ROW_DMA_UNROLL)

    def sum_rows(start_next):
        def body(r, c):
            rows = pl.ds(pl.multiple_of(r * SUBLANES, SUBLANES), SUBLANES)
            acc = x1_ref[rows, :]
            for k in range(TOP_K):
                acc = acc + buf[slot, k, rows, :] * wsm_ref[0, 0, k * ts + r]
            acc_ref[rows, :] = acc
            if start_next:
                issue_row(dnext_ref, 1 - slot, r)
            return c
        lax.fori_loop(0, ts, body, 0, unroll=ROW_DMA_UNROLL)

    @pl.when(i + 1 < n)
    def _():
        sum_rows(True)

    @pl.when(i + 1 == n)
    def _():
        sum_rows(False)

    acc = _load_rows(acc_ref, ts)
    gate = jax.nn.sigmoid(_dot(acc.astype(BF16), wg_ref[...]))
    p = jnp.where(i < n_prompt_tiles, pp_ref[...], ps_ref[...])
    proj = _dot(p.astype(BF16), wp_ref[...])
    y = _rms(acc + gate * proj, nfin_ref[...])

    @pl.when(i < n_prompt_tiles)
    def _():
        yp_ref[...] = y

    @pl.when(i >= n_prompt_tiles)
    def _():
        ysm_ref[...] = y


def _const_spec(shape):
    nd = len(shape)
    return pl.BlockSpec(shape, lambda *_: (0,) * nd)


def kernel(x_prompt, x_sample, state_pool, p_prompt, p_sample, norm_mix, w_in, w_pool, pool_scale,
           sgu_ln_g, sgu_ln_b, sgu_w, sgu_b, w_out, norm_ffn, w_router, b_router, w_up, b_up,
           w_down, b_down, w_ple_gate, w_ple_proj, norm_final):
    B, L, D = x_prompt.shape
    NB, NL, _ = x_sample.shape
    PW = state_pool.shape[-1]
    NE = w_router.shape[-1]
    DFF = w_down.shape[2]
    PLE = p_prompt.shape[-1]
    TP = B * L
    TSM = NB * NL
    T = TP + TSM
    assert w_in.shape[0] == 1, "single trunk layer"
    assert L % TS_MIX == 0 and TSM == TS_MIX and TS_MIX % CHUNK == 0
    assert T % TS_SCATTER == 0 and TP % TS_COMBINE == 0 and TSM % TS_COMBINE == 0
    assert D == SUBLANES * LANES, "a token row must fill exactly one vreg tile"
    assert state_pool.shape[2] == POOL_BUF and PW % len(POOL_WINDOWS) == 0 and PW % SGU_HEADS == 0

    nm = norm_mix[0].reshape(1, D)
    nf = norm_ffn[0].reshape(1, D)
    nfin = norm_final.reshape(1, D)
    win_bf = w_in[0].astype(BF16)
    wpool_bf = w_pool[0].astype(BF16)
    pscale = pool_scale[0].reshape(1, PW)
    lng = sgu_ln_g[0].reshape(1, PW)
    lnb = sgu_ln_b[0].reshape(1, PW)
    wout_bf = w_out[0].astype(BF16)
    wrt_f = w_router[0].T
    wrt_hi = wrt_f.astype(BF16)
    wrt = jnp.concatenate([wrt_hi, (wrt_f - wrt_hi.astype(F32)).astype(BF16)], axis=0)
    br = b_router[0].reshape(NE, 1)
    wg_bf = w_ple_gate[0].astype(BF16)
    wp_bf = w_ple_proj[0].astype(BF16)

    cparams = functools.partial(pltpu.CompilerParams, vmem_limit_bytes=VMEM_LIMIT_BYTES)
    NT = L // TS_MIX
    n_prompt_steps = B * NT
    xs_rows = x_sample.transpose(1, 0, 2).reshape(TSM, D)
    hist = state_pool[0].transpose(1, 0, 2)
    sw_small = sgu_w[0, :, :NL, :NL].reshape(-1)
    sb_small = sgu_b[0, :, :NL].reshape(-1)
    smem = pl.BlockSpec(memory_space=pltpu.SMEM)
    any_spec = pl.BlockSpec(memory_space=pl.ANY)
    last_prompt = n_prompt_steps - 1
    x1_all, h2_all, idx_t, w_t, rank_t, counts, pool_tail, zp_s, vn_s = pl.pallas_call(
        functools.partial(_mixer_body, tiles_per_prompt=NT),
        grid=(n_prompt_steps + 1,),
        in_specs=[
            pl.BlockSpec((TS_MIX, D), lambda i: (jnp.minimum(i, last_prompt), 0)),
            _const_spec((TSM, D)), _const_spec(hist.shape),
            _const_spec((1, D)), _const_spec(win_bf.shape), _const_spec(wpool_bf.shape),
            _const_spec((1, PW)), _const_spec((1, PW)), _const_spec((1, PW)),
            _const_spec(sgu_w[0].shape), _const_spec((CHUNK, SGU_HEADS)), smem, smem,
            _const_spec(wout_bf.shape), _const_spec((1, D)), _const_spec(wrt.shape),
            _const_spec((NE, 1)),
        ],
        out_specs=[
            pl.BlockSpec((TS_MIX * SUBLANES, LANES), lambda i: (i, 0)),
            pl.BlockSpec((TS_MIX * SUBLANES, LANES), lambda i: (i, 0)),
            pl.BlockSpec((TOP_K, TS_MIX), lambda i: (0, i)),
            pl.BlockSpec((TOP_K, TS_MIX), lambda i: (0, i)),
            pl.BlockSpec((TOP_K, TS_MIX), lambda i: (0, i)),
            _const_spec((NE, LANES)),
            pl.BlockSpec((1, CARRY_ROWS, PW), lambda i: (jnp.minimum(i, last_prompt) // NT, 0, 0)),
            _const_spec((NL, NB, PW)),
            _const_spec((NL, NB, PW)),
        ],
        out_shape=[
            jax.ShapeDtypeStruct((T * SUBLANES, LANES), F32),
            jax.ShapeDtypeStruct((T * SUBLANES, LANES), F32),
            jax.ShapeDtypeStruct((TOP_K, T), I32),
            jax.ShapeDtypeStruct((TOP_K, T), F32),
            jax.ShapeDtypeStruct((TOP_K, T), I32),
            jax.ShapeDtypeStruct((NE, LANES), F32),
            jax.ShapeDtypeStruct((B, CARRY_ROWS, PW), F32),
            jax.ShapeDtypeStruct((NL, NB, PW), F32),
            jax.ShapeDtypeStruct((NL, NB, PW), F32),
        ],
        scratch_shapes=[pltpu.VMEM((CARRY_ROWS, PW), F32), pltpu.VMEM((NE, LANES), F32),
                        pltpu.VMEM((TS_MIX, PW), BF16)],
        compiler_params=cparams(dimension_semantics=("arbitrary",)),
        name="mixer",
    )(x_prompt.reshape(TP, D), xs_rows, hist, nm, win_bf, wpool_bf, pscale, lng, lnb, sgu_w[0],
      sgu_b[0].T, sw_small, sb_small, wout_bf, nf, wrt, br)

    A = T * TOP_K
    n_blocks = -(-A // TM_MOE) + NE
    n_slots = n_blocks * TM_MOE
    cnt_i = counts[:, 0].astype(I32)
    padded = (cnt_i + TM_MOE - 1) // TM_MOE * TM_MOE
    pad_end = jnp.cumsum(padded)
    pad_start = pad_end - padded
    n_used = (pad_end[-1] // TM_MOE).astype(I32).reshape(1)
    blk = jnp.minimum(jnp.arange(n_blocks, dtype=I32), n_used[0] - 1)
    block_e = jnp.minimum(jnp.sum(pad_end[None, :] <= (blk * TM_MOE)[:, None], axis=1), NE - 1).astype(I32)
    expert_ids = jnp.arange(NE, dtype=I32)[:, None, None]
    dest = jnp.sum(jnp.where(idx_t[None] == expert_ids, pad_start[:, None, None], 0), axis=0) + rank_t
    used = padded > 0
    ordinal = jnp.cumsum(used.astype(I32)) - 1
    n_distinct = jnp.sum(used.astype(I32)).reshape(1)
    block_ord = jnp.sum(jnp.where(block_e[:, None] == expert_ids[:, 0, 0][None, :], ordinal[None, :], 0), axis=1)
    slots_j = jnp.minimum(jnp.arange(NE + 1, dtype=I32), n_distinct[0] - 1)
    expert_order = jnp.sum(jnp.where(used[None, :] & (ordinal[None, :] == slots_j[:, None]),
                                     expert_ids[:, 0, 0][None, :], 0), axis=1).astype(I32)

    onehot_be = block_e[:, None] == expert_ids[:, 0, 0][None, :]
    first_row = jnp.sum(jnp.where(onehot_be, pad_start[None, :], 0), axis=1)
    rows_left = jnp.sum(jnp.where(onehot_be, cnt_i[None, :], 0), axis=1) - (blk * TM_MOE - first_row)
    block_valid = jnp.where(jnp.arange(n_blocks, dtype=I32) < n_used[0],
                            jnp.clip(rows_left, 0, TM_MOE), 0).astype(I32)

    def per_tile(a, ts):
        return a.reshape(TOP_K, T // ts, ts).transpose(1, 0, 2).reshape(T // ts, 1, TOP_K * ts)

    xs_sorted = pl.pallas_call(
        _scatter_body,
        grid_spec=pltpu.PrefetchScalarGridSpec(
            num_scalar_prefetch=3,
            grid=(T // TS_SCATTER,),
            in_specs=[pl.BlockSpec((1, 1, TOP_K * TS_SCATTER), lambda i, *_: (i, 0, 0),
                                   memory_space=pltpu.SMEM),
                      pl.BlockSpec((TS_SCATTER * SUBLANES, LANES), lambda i, *_: (i, 0))],
            out_specs=any_spec,
            scratch_shapes=[pltpu.VMEM((TM_MOE * SUBLANES, LANES), F32), pltpu.SemaphoreType.DMA(()),
                            pltpu.SemaphoreType.DMA(())],
        ),
        out_shape=jax.ShapeDtypeStruct(((n_slots + TM_MOE) * SUBLANES, LANES), F32),
        compiler_params=cparams(dimension_semantics=("arbitrary",)),
        name="moe_scatter",
    )(pad_start, cnt_i, n_used, per_tile(dest, TS_SCATTER), h2_all)

    ys_sorted = pl.pallas_call(
        _experts_body,
        grid_spec=pltpu.PrefetchScalarGridSpec(
            num_scalar_prefetch=6,
            grid=(n_blocks,),
            in_specs=[
                pl.BlockSpec((TM_MOE * SUBLANES, LANES), lambda b, be, nu, *_: (jnp.minimum(b, nu[0] - 1), 0)),
                any_spec,
                pl.BlockSpec((1, 1, 2 * DFF), lambda b, be, *_: (be[b], 0, 0)),
                any_spec,
                pl.BlockSpec((1, 1, D), lambda b, be, *_: (be[b], 0, 0)),
            ],
            out_specs=pl.BlockSpec((TM_MOE * SUBLANES, LANES), lambda b, *_: (b, 0)),
            scratch_shapes=[pltpu.VMEM((WEIGHT_SLOTS, D, 2 * DFF), F32), pltpu.VMEM((WEIGHT_SLOTS, DFF, D), F32),
                            pltpu.SemaphoreType.DMA((WEIGHT_SLOTS, 2))],
        ),
        out_shape=jax.ShapeDtypeStruct((n_slots * SUBLANES, LANES), F32),
        compiler_params=cparams(dimension_semantics=("arbitrary",)),
        name="moe_experts",
    )(block_e, n_used, block_ord.astype(I32), expert_order, n_distinct, block_valid, xs_sorted, w_up[0],
      b_up[0].reshape(NE, 1, 2 * DFF), w_down[0], b_down[0].reshape(NE, 1, D))

    n_tiles = T // TS_COMBINE
    npt = TP // TS_COMBINE
    y_p, y_s = pl.pallas_call(
        functools.partial(_combine_body, n_prompt_tiles=npt),
        grid=(n_tiles,),
        in_specs=[
            pl.BlockSpec((1, 1, TOP_K * TS_COMBINE), lambda i: (i, 0, 0), memory_space=pltpu.SMEM),
            pl.BlockSpec((1, 1, TOP_K * TS_COMBINE), lambda i: (jnp.minimum(i + 1, n_tiles - 1), 0, 0),
                         memory_space=pltpu.SMEM),
            pl.BlockSpec((1, 1, TOP_K * TS_COMBINE), lambda i: (i, 0, 0), memory_space=pltpu.SMEM),
            any_spec,
            pl.BlockSpec((TS_COMBINE * SUBLANES, LANES), lambda i: (i, 0)),
            pl.BlockSpec((TS_COMBINE, PLE), lambda i: (jnp.minimum(i, npt - 1), 0)),
            pl.BlockSpec((TS_COMBINE, PLE), lambda i: (jnp.maximum(i - npt, 0), 0)),
            _const_spec(wg_bf.shape), _const_spec(wp_bf.shape), _const_spec((1, D)),
        ],
        out_specs=[
            pl.BlockSpec((TS_COMBINE, D), lambda i: (jnp.minimum(i, npt - 1), 0)),
            pl.BlockSpec((TS_COMBINE, D), lambda i: (jnp.maximum(i - npt, 0), 0)),
        ],
        out_shape=[jax.ShapeDtypeStruct((TP, D), F32), jax.ShapeDtypeStruct((TSM, D), F32)],
        scratch_shapes=[pltpu.VMEM((2, TOP_K, TS_COMBINE * SUBLANES, LANES), F32),
                        pltpu.VMEM((TS_COMBINE * SUBLANES, LANES), F32),
                        pltpu.SemaphoreType.DMA((2,))],
        compiler_params=cparams(dimension_semantics=("arbitrary",)),
        name="moe_combine",
    )(per_tile(dest, TS_COMBINE), per_tile(dest, TS_COMBINE), per_tile(w_t, TS_COMBINE), ys_sorted,
      x1_all, p_prompt[0].reshape(TP, PLE), p_sample[0].transpose(1, 0, 2).reshape(TSM, PLE),
      wg_bf, wp_bf, nfin)

    y_prompt = y_p.reshape(B, L, D)
    y_sample = y_s.reshape(NL, NB, D).transpose(1, 0, 2)
    new_pool_prompt = pool_tail[:, CARRY_ROWS - POOL_BUF:, :][None]
    new_pool_sample = jnp.concatenate([state_pool[0][:, NL:, :], zp_s.transpose(1, 0, 2)], axis=1)[None]
    new_sgu_v_sample = vn_s.transpose(1, 0, 2)[None]
    return (y_prompt, y_sample, new_pool_prompt, new_pool_sample, new_sgu_v_sample)
```

```python
import functools

import jax
import jax.numpy as jnp
from jax import lax
from jax.experimental import pallas as pl
from jax.experimental.pallas import tpu as pltpu

F32 = jnp.float32
BF16 = jnp.bfloat16
I32 = jnp.int32

POOL_WINDOWS = (2, 4, 8, 16)
POOL_BUF = max(POOL_WINDOWS) - 1
SGU_HEADS = 4
CHUNK = 128
TOP_K = 4
SWIGLU_LIMIT = 7.0
SWIGLU_ALPHA = 1.702
RMS_EPS = 1e-6
LN_EPS = 1e-5

SUBLANES = 8
LANES = 128
CARRY_ROWS = 16
TS_MIX = 512
TS_SCATTER = 512
DMA_PRIORITIES = 2
ROW_DMA_UNROLL = 8
WEIGHT_SLOTS = 3
TM_MOE = 512
FF_CHUNK = 512
ROW_VARIANTS = 4
TS_COMBINE = 512
VMEM_LIMIT_BYTES = 56 * 1024 * 1024


def _rms(x, g):
    return x * lax.rsqrt(jnp.mean(x * x, axis=-1, keepdims=True) + RMS_EPS) * g


def _dot(a, b):
    return jnp.dot(a, b, preferred_element_type=F32)


def _load_rows(ref, n_rows):
    return jnp.concatenate([ref[pl.ds(c, n_rows, stride=SUBLANES), :] for c in range(SUBLANES)], axis=-1)


def _store_rows(ref, val):
    n_rows = val.shape[0]
    for c in range(SUBLANES):
        ref[pl.ds(c, n_rows, stride=SUBLANES), :] = val[:, c * LANES:(c + 1) * LANES]


def _gelu(x):
    return 0.5 * x * (1.0 + lax.erf(x * (2.0 ** -0.5)))


def _head_layernorm(v, g, b, hd):
    outs = []
    for h in range(SGU_HEADS):
        vh = v[:, h * hd:(h + 1) * hd]
        mu = jnp.mean(vh, axis=-1, keepdims=True)
        d = vh - mu
        var = jnp.mean(d * d, axis=-1, keepdims=True)
        outs.append(d * lax.rsqrt(var + LN_EPS) * g[:, h * hd:(h + 1) * hd] + b[:, h * hd:(h + 1) * hd])
    return outs


def _pool_project(diffs, wpool_ref, pscale):
    gd = diffs[0].shape[-1]
    outs = []
    for g, d in enumerate(diffs):
        outs.append(_dot(d.astype(BF16), wpool_ref[g]) * pscale[:, g * gd:(g + 1) * gd])
    return jnp.concatenate(outs, axis=-1).astype(BF16)


def _route_and_store(x, a_bf, b_bf, wout_ref, nf_ref, wrs_ref, br_ref, cnt_ref,
                     x1_ref, h2_ref, idx_ref, w_ref, rank_ref, cnt_out_ref):
    ts = x.shape[0]
    pw = a_bf.shape[-1]
    n_exp = wrs_ref.shape[0] // 2
    mix = _dot(a_bf, wout_ref[:pw, :]) + _dot(b_bf, wout_ref[pw:, :])
    x1 = x + mix
    _store_rows(x1_ref, x1)
    h2 = _rms(x1, nf_ref[...])
    _store_rows(h2_ref, h2)
    h_hi = h2.astype(BF16)
    h_lo = (h2 - h_hi.astype(F32)).astype(BF16)
    nt = (((1,), (1,)), ((), ()))
    part = lax.dot_general(wrs_ref[...], h_hi, nt, preferred_element_type=F32)
    logits = (part[:n_exp] + part[n_exp:]
              + lax.dot_general(wrs_ref[:n_exp, :], h_lo, nt, preferred_element_type=F32) + br_ref[...])
    eio = lax.broadcasted_iota(I32, (n_exp, ts), 0).astype(F32)
    l = logits
    tops, sels = [], []
    for _ in range(TOP_K):
        m = jnp.max(l, axis=0, keepdims=True)
        sel = jnp.min(jnp.where(l == m, eio, float(n_exp)), axis=0, keepdims=True)
        tops.append(m)
        sels.append(sel)
        l = jnp.where(eio == sel, -jnp.inf, l)
    exps = [jnp.exp(t - tops[0]) for t in tops]
    denom = exps[0] + exps[1] + exps[2] + exps[3]
    onehot = jnp.zeros((n_exp, ts), F32)
    for sel in sels:
        onehot = onehot + jnp.where(eio == sel, 1.0, 0.0)
    r_i = lax.broadcasted_iota(I32, (ts, ts), 0)
    c_i = lax.broadcasted_iota(I32, (ts, ts), 1)
    upper = jnp.where(r_i < c_i, 1.0, 0.0).astype(BF16)
    base = cnt_ref[:, 0:1]
    rank_all = _dot(onehot.astype(BF16), upper) + base
    for k in range(TOP_K):
        idx_ref[k:k + 1, :] = sels[k].astype(I32)
        w_ref[k:k + 1, :] = exps[k] / denom
        rk = jnp.sum(jnp.where(eio == sels[k], rank_all, 0.0), axis=0, keepdims=True)
        rank_ref[k:k + 1, :] = rk.astype(I32)
    cnt_ref[...] = cnt_ref[...] + jnp.sum(onehot, axis=1, keepdims=True)
    cnt_out_ref[...] = cnt_ref[...]


def _prompt_tile(t, x_ref, nm_ref, win_ref, wpool_ref, pscale_ref, lng_ref, lnb_ref, sw_ref, sbt_ref,
                 tail_ref, carry_ref, bo_ref, n_tiles, route):
    ts = x_ref.shape[0]
    pw = pscale_ref.shape[-1]
    gd = pw // len(POOL_WINDOWS)
    hd = pw // SGU_HEADS

    @pl.when(t == 0)
    def _():
        carry_ref[...] = jnp.zeros_like(carry_ref)

    x = x_ref[...]
    h = _rms(x, nm_ref[...])
    z = _dot(h.astype(BF16), win_ref[...])
    zp = z[:, :pw]

    ext = jnp.concatenate([carry_ref[...], zp], axis=0)
    carry_ref[...] = zp[ts - CARRY_ROWS:, :]

    @pl.when(t == n_tiles - 1)
    def _():
        tail_ref[0] = zp[ts - CARRY_ROWS:, :]

    pos = t * ts + lax.broadcasted_iota(I32, (ts, 1), 0)
    diffs = []
    for g, w in enumerate(POOL_WINDOWS):
        acc = ext[:, g * gd:(g + 1) * gd]
        k = 1
        while k < w:
            acc = acc + pltpu.roll(acc, k, 0)
            k *= 2
        cnt = jnp.minimum(w, pos + 1).astype(F32)
        diffs.append(acc[CARRY_ROWS:, :] / cnt - zp[:, g * gd:(g + 1) * gd])
    a_bf = _pool_project(diffs, wpool_ref, pscale_ref[...])

    zuv = _gelu(z[:, pw:])
    u = zuv[:, :pw]
    vn = _head_layernorm(zuv[:, pw:], lng_ref[...], lnb_ref[...], hd)
    ri = lax.broadcasted_iota(I32, (CHUNK, CHUNK), 0)
    ci = lax.broadcasted_iota(I32, (CHUNK, CHUNK), 1)
    for hh in range(SGU_HEADS):
        wm = jnp.where(ci <= ri, sw_ref[hh], 0.0).astype(BF16)
        bias = sbt_ref[:, hh:hh + 1]
        vh = vn[hh].astype(BF16)
        for c in range(ts // CHUNK):
            rows = slice(c * CHUNK, (c + 1) * CHUNK)
            mixed = _dot(wm, vh[rows, :]) + bias
            bo_ref[rows, hh * hd:(hh + 1) * hd] = (u[rows, hh * hd:(hh + 1) * hd] * mixed).astype(BF16)
    route(x, a_bf, bo_ref[...])


def _sample_tile(x_ref, st_ref, nm_ref, win_ref, wpool_ref, pscale_ref, lng_ref, lnb_ref, sw_ref,
                 sb_ref, zp_ref, vn_ref, route):
    n_new, nb, pw = zp_ref.shape
    gd = pw // len(POOL_WINDOWS)
    hd = pw // SGU_HEADS

    x = x_ref[...]
    h = _rms(x, nm_ref[...])
    z = _dot(h.astype(BF16), win_ref[...])
    zp = z[:, :pw]
    for l in range(n_new):
        zp_ref[l] = zp[l * nb:(l + 1) * nb, :]

    diffs = []
    for g, w in enumerate(POOL_WINDOWS):
        cols = slice(g * gd, (g + 1) * gd)
        lo_needed = POOL_BUF + 1 - w
        suffix = {}
        run = None
        for r in range(POOL_BUF - 1, lo_needed - 1, -1):
            slab = st_ref[r][:, cols]
            run = slab if run is None else run + slab
            suffix[r] = run
        pieces = []
        for l in range(n_new):
            first_hist = POOL_BUF + 1 + l - w
            tot = suffix[first_hist] if first_hist < POOL_BUF else None
            for l2 in range(max(0, l - w + 1), l + 1):
                zl = zp[l2 * nb:(l2 + 1) * nb, cols]
                tot = zl if tot is None else tot + zl
            pieces.append(tot / float(w) - zp[l * nb:(l + 1) * nb, cols])
        diffs.append(jnp.concatenate(pieces, axis=0))
    a_bf = _pool_project(diffs, wpool_ref, pscale_ref[...])

    zuv = _gelu(z[:, pw:])
    u = zuv[:, :pw]
    vn = _head_layernorm(zuv[:, pw:], lng_ref[...], lnb_ref[...], hd)
    vfull = jnp.concatenate(vn, axis=-1)
    for l in range(n_new):
        vn_ref[l] = vfull[l * nb:(l + 1) * nb, :]
    b_cols = []
    for hh in range(SGU_HEADS):
        pieces = []
        for l in range(n_new):
            mixed = None
            for l2 in range(l + 1):
                term = sw_ref[(hh * n_new + l) * n_new + l2] * vn[hh][l2 * nb:(l2 + 1) * nb, :]
                mixed = term if mixed is None else mixed + term
            mixed = mixed + sb_ref[hh * n_new + l]
            pieces.append(u[l * nb:(l + 1) * nb, hh * hd:(hh + 1) * hd] * mixed)
        b_cols.append(jnp.concatenate(pieces, axis=0))
    route(x, a_bf, jnp.concatenate(b_cols, axis=-1).astype(BF16))


def _mixer_body(xp_ref, xs_ref, st_ref, nm_ref, win_ref, wpool_ref, pscale_ref, lng_ref, lnb_ref,
                sw_ref, sbt_ref, sws_ref, sbs_ref, wout_ref, nf_ref, wrs_ref, br_ref,
                x1_ref, h2_ref, idx_ref, w_ref, rank_ref, cnt_out_ref, tail_ref, zp_ref, vn_ref,
                carry_ref, cnt_ref, bo_ref, *, tiles_per_prompt):
    i = pl.program_id(0)
    n_prompt_steps = pl.num_programs(0) - 1

    @pl.when(i == 0)
    def _():
        cnt_ref[...] = jnp.zeros_like(cnt_ref)

    route = functools.partial(_route_and_store, wout_ref=wout_ref, nf_ref=nf_ref, wrs_ref=wrs_ref,
                              br_ref=br_ref, cnt_ref=cnt_ref, x1_ref=x1_ref, h2_ref=h2_ref,
                              idx_ref=idx_ref,
                              w_ref=w_ref, rank_ref=rank_ref, cnt_out_ref=cnt_out_ref)

    @pl.when(i < n_prompt_steps)
    def _():
        _prompt_tile(lax.rem(i, tiles_per_prompt), xp_ref, nm_ref, win_ref, wpool_ref, pscale_ref,
                     lng_ref, lnb_ref, sw_ref, sbt_ref, tail_ref, carry_ref, bo_ref,
                     tiles_per_prompt, route)

    @pl.when(i == n_prompt_steps)
    def _():
        _sample_tile(xs_ref, st_ref, nm_ref, win_ref, wpool_ref, pscale_ref, lng_ref, lnb_ref,
                     sws_ref, sbs_ref, zp_ref, vn_ref, route)


def _row_copy(src, src_row, dst, dst_row, sem):
    s_rows = pl.ds(pl.multiple_of(src_row * SUBLANES, SUBLANES), SUBLANES)
    d_rows = pl.ds(pl.multiple_of(dst_row * SUBLANES, SUBLANES), SUBLANES)
    return pltpu.make_async_copy(src.at[s_rows], dst.at[d_rows], sem)


def _scatter_body(pstart_ref, pcnt_ref, nused_ref, dest_ref, x1_ref, xs_hbm, zero_ref, zsem, rsem):
    i = pl.program_id(0)
    ts = x1_ref.shape[0] // SUBLANES
    tm = zero_ref.shape[0] // SUBLANES
    n_exp = pstart_ref.shape[0]

    def zero_block(row):
        rows = pl.ds(pl.multiple_of(row * SUBLANES, SUBLANES), tm * SUBLANES)
        return pltpu.make_async_copy(zero_ref, xs_hbm.at[rows], zsem)

    @pl.when(i == 0)
    def _():
        zero_ref[...] = jnp.zeros_like(zero_ref)
        for e in range(n_exp):
            zero_block(pstart_ref[e] + pcnt_ref[e]).start()
        for e in range(n_exp):
            zero_block(pstart_ref[e] + pcnt_ref[e]).wait()

        n_total = xs_hbm.shape[0] // (tm * SUBLANES)

        def start_tail(b, c):
            zero_block(b * tm).start()
            return c

        def wait_tail(b, c):
            zero_block(b * tm).wait()
            return c

        lax.fori_loop(nused_ref[0], n_total, start_tail, 0)
        lax.fori_loop(nused_ref[0], n_total, wait_tail, 0)

    def start_rows(r, carry):
        for k in range(TOP_K):
            _row_copy(x1_ref, r, xs_hbm, dest_ref[0, 0, k * ts + r], rsem).start(priority=k % DMA_PRIORITIES)
        return carry

    lax.fori_loop(0, ts, start_rows, 0, unroll=ROW_DMA_UNROLL)

    def wait_rows(r, carry):
        for k in range(TOP_K):
            _row_copy(x1_ref, 0, xs_hbm, 0, rsem).wait()
        return carry

    lax.fori_loop(0, ts, wait_rows, 0, unroll=ROW_DMA_UNROLL)


def _experts_body(be_ref, nused_ref, eord_ref, eorder_ref, ndist_ref, bvalid_ref,
                  xs_ref, wup_hbm, bup_ref, wdn_hbm, bdn_ref, ys_ref,
                  wup_f32, wdn_f32, wsem):
    b = pl.program_id(0)
    d_ff = wdn_f32.shape[1]

    def weight_copies(e, slot):
        return (pltpu.make_async_copy(wup_hbm.at[e], wup_f32.at[slot], wsem.at[slot, 0]),
                pltpu.make_async_copy(wdn_hbm.at[e], wdn_f32.at[slot], wsem.at[slot, 1]))

    lookahead = WEIGHT_SLOTS - 1

    @pl.when(b == 0)
    def _():
        for a in range(lookahead):
            @pl.when(a < ndist_ref[0])
            def _(a=a):
                for cp in weight_copies(eorder_ref[a], a):
                    cp.start()

    @pl.when(b < nused_ref[0])
    def _():
        prev = be_ref[jnp.maximum(b - 1, 0)]

        @pl.when((b == 0) | (be_ref[b] != prev))
        def _():
            j = eord_ref[b]
            slot = lax.rem(j, WEIGHT_SLOTS)
            for cp in weight_copies(be_ref[b], slot):
                cp.wait()

            @pl.when(j + lookahead < ndist_ref[0])
            def _():
                for cp in weight_copies(eorder_ref[j + lookahead], lax.rem(j + lookahead, WEIGHT_SLOTS)):
                    cp.start()

        wslot = lax.rem(eord_ref[b], WEIGHT_SLOTS)

        def compute(n_rows):
            xb = _load_rows(xs_ref, n_rows).astype(BF16)
            acc = None
            for j in range(0, d_ff, FF_CHUNK):
                glu = _dot(xb, wup_f32[wslot, :, j:j + FF_CHUNK].astype(BF16)) + bup_ref[0, :, j:j + FF_CHUNK]
                lin = (_dot(xb, wup_f32[wslot, :, d_ff + j:d_ff + j + FF_CHUNK].astype(BF16))
                       + bup_ref[0, :, d_ff + j:d_ff + j + FF_CHUNK])
                glu = jnp.minimum(glu, SWIGLU_LIMIT)
                lin = jnp.clip(lin, -SWIGLU_LIMIT, SWIGLU_LIMIT)
                act = glu * jax.nn.sigmoid(SWIGLU_ALPHA * glu) * (lin + 1.0)
                part = _dot(act.astype(BF16), wdn_f32[wslot, j:j + FF_CHUNK, :].astype(BF16))
                acc = part if acc is None else acc + part
            _store_rows(ys_ref, acc + bdn_ref[0])
            if n_rows * SUBLANES < ys_ref.shape[0]:
                ys_ref[n_rows * SUBLANES:, :] = jnp.zeros(
                    (ys_ref.shape[0] - n_rows * SUBLANES, ys_ref.shape[1]), ys_ref.dtype)

        tm = ys_ref.shape[0] // SUBLANES
        step = tm // ROW_VARIANTS
        valid = bvalid_ref[b]
        for hi in range(step, tm + 1, step):
            lo = hi - step if hi > step else -1
            @pl.when((valid > lo) & (valid <= hi))
            def _(hi=hi):
                compute(hi)

    @pl.when(b >= nused_ref[0])
    def _():
        ys_ref[...] = jnp.zeros_like(ys_ref)


def _combine_body(dcur_ref, dnext_ref, wsm_ref, ys_hbm, x1_ref, pp_ref, ps_ref, wg_ref, wp_ref, nfin_ref,
                  yp_ref, ysm_ref, buf, acc_ref, sem, *, n_prompt_tiles):
    i = pl.program_id(0)
    n = pl.num_programs(0)
    ts = x1_ref.shape[0] // SUBLANES
    slot = i % 2

    def issue_row(dref, sl, r):
        for k in range(TOP_K):
            cp = _row_copy(ys_hbm, dref[0, 0, k * ts + r], buf.at[sl, k], r, sem.at[sl])
            cp.start(priority=k % DMA_PRIORITIES)

    @pl.when(i == 0)
    def _():
        def body(r, c):
            issue_row(dcur_ref, 0, r)
            return c
        lax.fori_loop(0, ts, body, 0, unroll=ROW_DMA_UNROLL)

    def wait_rows(r, c):
        for k in range(TOP_K):
            _row_copy(ys_hbm, 0, buf.at[slot, k], 0, sem.at[slot]).wait()
        return c

    lax.fori_loop(0, ts, wait_rows, 0, unroll=ROW_DMA_UNROLL)

    def sum_rows(start_next):
        def body(r, c):
            rows = pl.ds(pl.multiple_of(r * SUBLANES, SUBLANES), SUBLANES)
            acc = x1_ref[rows, :]
            for k in range(TOP_K):
                acc = acc + buf[slot, k, rows, :] * wsm_ref[0, 0, k * ts + r]
            acc_ref[rows, :] = acc
            if start_next:
                issue_row(dnext_ref, 1 - slot, r)
            return c
        lax.fori_loop(0, ts, body, 0, unroll=ROW_DMA_UNROLL)

    @pl.when(i + 1 < n)
    def _():
        sum_rows(True)

    @pl.when(i + 1 == n)
    def _():
        sum_rows(False)

    acc = _load_rows(acc_ref, ts)
    gate = jax.nn.sigmoid(_dot(acc.astype(BF16), wg_ref[...]))
    p = jnp.where(i < n_prompt_tiles, pp_ref[...], ps_ref[...])
    proj = _dot(p.astype(BF16), wp_ref[...])
    y = _rms(acc + gate * proj, nfin_ref[...])

    @pl.when(i < n_prompt_tiles)
    def _():
        yp_ref[...] = y

    @pl.when(i >= n_prompt_tiles)
    def _():
        ysm_ref[...] = y


def _const_spec(shape):
    nd = len(shape)
    return pl.BlockSpec(shape, lambda *_: (0,) * nd)


def kernel(x_prompt, x_sample, state_pool, p_prompt, p_sample, norm_mix, w_in, w_pool, pool_scale,
           sgu_ln_g, sgu_ln_b, sgu_w, sgu_b, w_out, norm_ffn, w_router, b_router, w_up, b_up,
           w_down, b_down, w_ple_gate, w_ple_proj, norm_final):
    B, L, D = x_prompt.shape
    NB, NL, _ = x_sample.shape
    PW = state_pool.shape[-1]
    NE = w_router.shape[-1]
    DFF = w_down.shape[2]
    PLE = p_prompt.shape[-1]
    TP = B * L
    TSM = NB * NL
    T = TP + TSM
    assert w_in.shape[0] == 1, "single trunk layer"
    assert L % TS_MIX == 0 and TSM == TS_MIX and TS_MIX % CHUNK == 0
    assert T % TS_SCATTER == 0 and TP % TS_COMBINE == 0 and TSM % TS_COMBINE == 0
    assert D == SUBLANES * LANES, "a token row must fill exactly one vreg tile"
    assert state_pool.shape[2] == POOL_BUF and PW % len(POOL_WINDOWS) == 0 and PW % SGU_HEADS == 0

    nm = norm_mix[0].reshape(1, D)
    nf = norm_ffn[0].reshape(1, D)
    nfin = norm_final.reshape(1, D)
    win_bf = w_in[0].astype(BF16)
    wpool_bf = w_pool[0].astype(BF16)
    pscale = pool_scale[0].reshape(1, PW)
    lng = sgu_ln_g[0].reshape(1, PW)
    lnb = sgu_ln_b[0].reshape(1, PW)
    wout_bf = w_out[0].astype(BF16)
    wrt_f = w_router[0].T
    wrt_hi = wrt_f.astype(BF16)
    wrt = jnp.concatenate([wrt_hi, (wrt_f - wrt_hi.astype(F32)).astype(BF16)], axis=0)
    br = b_router[0].reshape(NE, 1)
    wg_bf = w_ple_gate[0].astype(BF16)
    wp_bf = w_ple_proj[0].astype(BF16)

    cparams = functools.partial(pltpu.CompilerParams, vmem_limit_bytes=VMEM_LIMIT_BYTES)
    NT = L // TS_MIX
    n_prompt_steps = B * NT
    xs_rows = x_sample.transpose(1, 0, 2).reshape(TSM, D)
    hist = state_pool[0].transpose(1, 0, 2)
    sw_small = sgu_w[0, :, :NL, :NL].reshape(-1)
    sb_small = sgu_b[0, :, :NL].reshape(-1)
    smem = pl.BlockSpec(memory_space=pltpu.SMEM)
    any_spec = pl.BlockSpec(memory_space=pl.ANY)
    last_prompt = n_prompt_steps - 1
    x1_all, h2_all, idx_t, w_t, rank_t, counts, pool_tail, zp_s, vn_s = pl.pallas_call(
        functools.partial(_mixer_body, tiles_per_prompt=NT),
        grid=(n_prompt_steps + 1,),
        in_specs=[
            pl.BlockSpec((TS_MIX, D), lambda i: (jnp.minimum(i, last_prompt), 0)),
            _const_spec((TSM, D)), _const_spec(hist.shape),
            _const_spec((1, D)), _const_spec(win_bf.shape), _const_spec(wpool_bf.shape),
            _const_spec((1, PW)), _const_spec((1, PW)), _const_spec((1, PW)),
            _const_spec(sgu_w[0].shape), _const_spec((CHUNK, SGU_HEADS)), smem, smem,
            _const_spec(wout_bf.shape), _const_spec((1, D)), _const_spec(wrt.shape),
            _const_spec((NE, 1)),
        ],
        out_specs=[
            pl.BlockSpec((TS_MIX * SUBLANES, LANES), lambda i: (i, 0)),
            pl.BlockSpec((TS_MIX * SUBLANES, LANES), lambda i: (i, 0)),
            pl.BlockSpec((TOP_K, TS_MIX), lambda i: (0, i)),
            pl.BlockSpec((TOP_K, TS_MIX), lambda i: (0, i)),
            pl.BlockSpec((TOP_K, TS_MIX), lambda i: (0, i)),
            _const_spec((NE, LANES)),
            pl.BlockSpec((1, CARRY_ROWS, PW), lambda i: (jnp.minimum(i, last_prompt) // NT, 0, 0)),
            _const_spec((NL, NB, PW)),
            _const_spec((NL, NB, PW)),
        ],
        out_shape=[
            jax.ShapeDtypeStruct((T * SUBLANES, LANES), F32),
            jax.ShapeDtypeStruct((T * SUBLANES, LANES), F32),
            jax.ShapeDtypeStruct((TOP_K, T), I32),
            jax.ShapeDtypeStruct((TOP_K, T), F32),
            jax.ShapeDtypeStruct((TOP_K, T), I32),
            jax.ShapeDtypeStruct((NE, LANES), F32),
            jax.ShapeDtypeStruct((B, CARRY_ROWS, PW), F32),
            jax.ShapeDtypeStruct((NL, NB, PW), F32),
            jax.ShapeDtypeStruct((NL, NB, PW), F32),
        ],
        scratch_shapes=[pltpu.VMEM((CARRY_ROWS, PW), F32), pltpu.VMEM((NE, LANES), F32),
                        pltpu.VMEM((TS_MIX, PW), BF16)],
        compiler_params=cparams(dimension_semantics=("arbitrary",)),
        name="mixer",
    )(x_prompt.reshape(TP, D), xs_rows, hist, nm, win_bf, wpool_bf, pscale, lng, lnb, sgu_w[0],
      sgu_b[0].T, sw_small, sb_small, wout_bf, nf, wrt, br)

    A = T * TOP_K
    n_blocks = -(-A // TM_MOE) + NE
    n_slots = n_blocks * TM_MOE
    cnt_i = counts[:, 0].astype(I32)
    padded = (cnt_i + TM_MOE - 1) // TM_MOE * TM_MOE
    pad_end = jnp.cumsum(padded)
    pad_start = pad_end - padded
    n_used = (pad_end[-1] // TM_MOE).astype(I32).reshape(1)
    blk = jnp.minimum(jnp.arange(n_blocks, dtype=I32), n_used[0] - 1)
    block_e = jnp.minimum(jnp.sum(pad_end[None, :] <= (blk * TM_MOE)[:, None], axis=1), NE - 1).astype(I32)
    expert_ids = jnp.arange(NE, dtype=I32)[:, None, None]
    dest = jnp.sum(jnp.where(idx_t[None] == expert_ids, pad_start[:, None, None], 0), axis=0) + rank_t
    used = padded > 0
    ordinal = jnp.cumsum(used.astype(I32)) - 1
    n_distinct = jnp.sum(used.astype(I32)).reshape(1)
    block_ord = jnp.sum(jnp.where(block_e[:, None] == expert_ids[:, 0, 0][None, :], ordinal[None, :], 0), axis=1)
    slots_j = jnp.minimum(jnp.arange(NE + 1, dtype=I32), n_distinct[0] - 1)
    expert_order = jnp.sum(jnp.where(used[None, :] & (ordinal[None, :] == slots_j[:, None]),
                                     expert_ids[:, 0, 0][None, :], 0), axis=1).astype(I32)

    onehot_be = block_e[:, None] == expert_ids[:, 0, 0][None, :]
    first_row = jnp.sum(jnp.where(onehot_be, pad_start[None, :], 0), axis=1)
    rows_left = jnp.sum(jnp.where(onehot_be, cnt_i[None, :], 0), axis=1) - (blk * TM_MOE - first_row)
    block_valid = jnp.where(jnp.arange(n_blocks, dtype=I32) < n_used[0],
                            jnp.clip(rows_left, 0, TM_MOE), 0).astype(I32)

    def per_tile(a, ts):
        return a.reshape(TOP_K, T // ts, ts).transpose(1, 0, 2).reshape(T // ts, 1, TOP_K * ts)

    xs_sorted = pl.pallas_call(
        _scatter_body,
        grid_spec=pltpu.PrefetchScalarGridSpec(
            num_scalar_prefetch=3,
            grid=(T // TS_SCATTER,),
            in_specs=[pl.BlockSpec((1, 1, TOP_K * TS_SCATTER), lambda i, *_: (i, 0, 0),
                                   memory_space=pltpu.SMEM),
                      pl.BlockSpec((TS_SCATTER * SUBLANES, LANES), lambda i, *_: (i, 0))],
            out_specs=any_spec,
            scratch_shapes=[pltpu.VMEM((TM_MOE * SUBLANES, LANES), F32), pltpu.SemaphoreType.DMA(()),
                            pltpu.SemaphoreType.DMA(())],
        ),
        out_shape=jax.ShapeDtypeStruct(((n_slots + TM_MOE) * SUBLANES, LANES), F32),
        compiler_params=cparams(dimension_semantics=("arbitrary",)),
        name="moe_scatter",
    )(pad_start, cnt_i, n_used, per_tile(dest, TS_SCATTER), h2_all)

    ys_sorted = pl.pallas_call(
        _experts_body,
        grid_spec=pltpu.PrefetchScalarGridSpec(
            num_scalar_prefetch=6,
            grid=(n_blocks,),
            in_specs=[
                pl.BlockSpec((TM_MOE * SUBLANES, LANES), lambda b, be, nu, *_: (jnp.minimum(b, nu[0] - 1), 0)),
                any_spec,
                pl.BlockSpec((1, 1, 2 * DFF), lambda b, be, *_: (be[b], 0, 0)),
                any_spec,
                pl.BlockSpec((1, 1, D), lambda b, be, *_: (be[b], 0, 0)),
            ],
            out_specs=pl.BlockSpec((TM_MOE * SUBLANES, LANES), lambda b, *_: (b, 0)),
            scratch_shapes=[pltpu.VMEM((WEIGHT_SLOTS, D, 2 * DFF), F32), pltpu.VMEM((WEIGHT_SLOTS, DFF, D), F32),
                            pltpu.SemaphoreType.DMA((WEIGHT_SLOTS, 2))],
        ),
        out_shape=jax.ShapeDtypeStruct((n_slots * SUBLANES, LANES), F32),
        compiler_params=cparams(dimension_semantics=("arbitrary",)),
        name="moe_experts",
    )(block_e, n_used, block_ord.astype(I32), expert_order, n_distinct, block_valid, xs_sorted, w_up[0],
      b_up[0].reshape(NE, 1, 2 * DFF), w_down[0], b_down[0].reshape(NE, 1, D))

    n_tiles = T // TS_COMBINE
    npt = TP // TS_COMBINE
    y_p, y_s = pl.pallas_call(
        functools.partial(_combine_body, n_prompt_tiles=npt),
        grid=(n_tiles,),
        in_specs=[
            pl.BlockSpec((1, 1, TOP_K * TS_COMBINE), lambda i: (i, 0, 0), memory_space=pltpu.SMEM),
            pl.BlockSpec((1, 1, TOP_K * TS_COMBINE), lambda i: (jnp.minimum(i + 1, n_tiles - 1), 0, 0),
                         memory_space=pltpu.SMEM),
            pl.BlockSpec((1, 1, TOP_K * TS_COMBINE), lambda i: (i, 0, 0), memory_space=pltpu.SMEM),
            any_spec,
            pl.BlockSpec((TS_COMBINE * SUBLANES, LANES), lambda i: (i, 0)),
            pl.BlockSpec((TS_COMBINE, PLE), lambda i: (jnp.minimum(i, npt - 1), 0)),
            pl.BlockSpec((TS_COMBINE, PLE), lambda i: (jnp.maximum(i - npt, 0), 0)),
            _const_spec(wg_bf.shape), _const_spec(wp_bf.shape), _const_spec((1, D)),
        ],
        out_specs=[
            pl.BlockSpec((TS_COMBINE, D), lambda i: (jnp.minimum(i, npt - 1), 0)),
            pl.BlockSpec((TS_COMBINE, D), lambda i: (jnp.maximum(i - npt, 0), 0)),
        ],
        out_shape=[jax.ShapeDtypeStruct((TP, D), F32), jax.ShapeDtypeStruct((TSM, D), F32)],
        scratch_shapes=[pltpu.VMEM((2, TOP_K, TS_COMBINE * SUBLANES, LANES), F32),
                        pltpu.VMEM((TS_COMBINE * SUBLANES, LANES), F32),
                        pltpu.SemaphoreType.DMA((2,))],
        compiler_params=cparams(dimension_semantics=("arbitrary",)),
        name="moe_combine",
    )(per_tile(dest, TS_COMBINE), per_tile(dest, TS_COMBINE), per_tile(w_t, TS_COMBINE), ys_sorted,
      x1_all, p_prompt[0].reshape(TP, PLE), p_sample[0].transpose(1, 0, 2).reshape(TSM, PLE),
      wg_bf, wp_bf, nfin)

    y_prompt = y_p.reshape(B, L, D)
    y_sample = y_s.reshape(NL, NB, D).transpose(1, 0, 2)
    new_pool_prompt = pool_tail[:, CARRY_ROWS - POOL_BUF:, :][None]
    new_pool_sample = jnp.concatenate([state_pool[0][:, NL:, :], zp_s.transpose(1, 0, 2)], axis=1)[None]
    new_sgu_v_sample = vn_s.transpose(1, 0, 2)[None]
    return (y_prompt, y_sample, new_pool_prompt, new_pool_sample, new_sgu_v_sample)
```
